```python
import math
import jax, jax.numpy as jnp
from jax import lax
import numpy as np

D_MODEL = 1024
BATCH = 2
SEQ = 8192
DEPTH = 2

QBLK = 128
NORM_EPS = 1e-6
ROPE_BASE = 10000.0
MAX_POS_OFFSET = 4096

A_HEADS = 8
A_HEAD_DIM = 64
IDX_HEADS = 4
IDX_DIM = 64
DSA_TOPK = 256
A_WIDTH = A_HEADS * A_HEAD_DIM
B_HEADS = 4
B_QK_DIM = 64
B_V_DIM = 128
RET_CHUNK = 128
B_WIDTH = B_HEADS * B_V_DIM
C_HEADS = 4
C_QK_DIM = 64
C_V_DIM = 128
C_WIDTH = C_HEADS * C_V_DIM
D_HEADS = 8
D_NOPE_DIM = 64
D_ROPE_DIM = 32
D_V_DIM = 64
D_Q_LORA = 256
D_KV_LORA = 128
D_WIDTH = D_HEADS * D_V_DIM

EVEN_MIX_WIDTH = A_WIDTH + B_WIDTH
ODD_MIX_WIDTH = C_WIDTH + D_WIDTH
EVEN_WIDTHS = (A_WIDTH, A_WIDTH, A_WIDTH, A_WIDTH,
               IDX_HEADS * IDX_DIM, IDX_DIM, IDX_HEADS,
               B_HEADS * B_QK_DIM, B_HEADS * B_QK_DIM,
               B_WIDTH, B_WIDTH)
ODD_WIDTHS = (C_HEADS * 2 * C_QK_DIM, C_HEADS * 2 * C_QK_DIM,
              C_WIDTH, C_WIDTH,
              D_Q_LORA, D_KV_LORA, D_ROPE_DIM, D_WIDTH)
EVEN_IN = sum(EVEN_WIDTHS)
ODD_IN = sum(ODD_WIDTHS)

kernel_name = 'hybrid_dsa_retention_diffattn_mla'


def split_columns(h, widths):
    out, start = [], 0
    for w in widths:
        out.append(h[..., start:start + w])
        start += w
    return out


def rms_norm(x, gain=None):
    xf = x.astype(jnp.float32)
    y = xf * lax.rsqrt(jnp.mean(xf * xf, axis=-1, keepdims=True) + NORM_EPS)
    if gain is not None:
        y = y * gain.astype(jnp.float32)
    return y.astype(x.dtype)


def head_group_norm(x):
    xf = x.astype(jnp.float32)
    mu = jnp.mean(xf, axis=-1, keepdims=True)
    var = jnp.mean(jnp.square(xf - mu), axis=-1, keepdims=True)
    return ((xf - mu) * lax.rsqrt(var + NORM_EPS)).astype(x.dtype)


def rotary(x, positions):
    half = x.shape[-1] // 2
    inv_freq = ROPE_BASE ** (-jnp.arange(half, dtype=jnp.float32) / half)
    ang = positions.astype(jnp.float32)[:, :, None, None] * inv_freq
    cos, sin = jnp.cos(ang), jnp.sin(ang)
    xf = x.astype(jnp.float32)
    x1, x2 = xf[..., :half], xf[..., half:]
    return jnp.concatenate([x1 * cos - x2 * sin, x1 * sin + x2 * cos], axis=-1).astype(x.dtype)


def to_blocks(a, nb):
    b = a.shape[0]
    return a.reshape((b, nb, QBLK) + a.shape[2:]).swapaxes(0, 1)


def from_blocks(a):
    nb, b, q = a.shape[:3]
    return a.swapaxes(0, 1).reshape((b, nb * q) + a.shape[3:])


def causal_attention(q, k, v, scale):
    t = q.shape[1]
    nb = t // QBLK
    kpos = jnp.arange(t)

    def block(args):
        qb, i = args
        qpos = i * QBLK + jnp.arange(QBLK)
        s = jnp.einsum('bqhd,bkhd->bhqk', qb, k).astype(jnp.float32) * scale
        s = jnp.where((kpos[None, :] <= qpos[:, None])[None, None], s, -jnp.inf)
        p = jax.nn.softmax(s, axis=-1).astype(v.dtype)
        return jnp.einsum('bhqk,bkhd->bqhd', p, v)

    out = lax.map(block, (to_blocks(q, nb), jnp.arange(nb)))
    return from_blocks(out)


def dsa_attention(q, k, v, q_idx, k_idx, w_idx):
    b, t, h, d = q.shape
    topk = min(DSA_TOPK, t // 4)
    nb = t // QBLK
    kpos = jnp.arange(t)
    w_idx = w_idx * (IDX_HEADS ** -0.5 * IDX_DIM ** -0.5)
    gather = jax.vmap(lambda a, ix: a[ix])

    def block(args):
        qb, qib, wb, i = args
        qpos = i * QBLK + jnp.arange(QBLK)
        causal = kpos[None, :] <= qpos[:, None]
        rel = jax.nn.relu(jnp.einsum('bqhd,bsd->bqhs', qib, k_idx))
        score = jnp.einsum('bqh,bqhs->bqs', wb, rel).astype(jnp.float32)
        score = jnp.where(causal[None], score, -jnp.inf)
        _, sel = lax.top_k(score, topk)
        valid = sel <= qpos[None, :, None]
        ks = gather(k, sel).reshape(b, QBLK, topk, h, d)
        vs = gather(v, sel).reshape(b, QBLK, topk, h, d)
        s = jnp.einsum('bqhd,bqkhd->bqhk', qb, ks).astype(jnp.float32) * (d ** -0.5)
        s = jnp.where(valid[:, :, None, :], s, -jnp.inf)
        p = jax.nn.softmax(s, axis=-1).astype(vs.dtype)
        return jnp.einsum('bqhk,bqkhd->bqhd', p, vs)

    out = lax.map(block, (to_blocks(q, nb), to_blocks(q_idx, nb), to_blocks(w_idx, nb), jnp.arange(nb)))
    return from_blocks(out).reshape(b, t, h * d)


def retention(q, k, v):
    b, t, h, dk = q.shape
    dv = v.shape[-1]
    c = RET_CHUNK
    n = t // c
    gammas = 1.0 - 2.0 ** (-5.0 - jnp.arange(h, dtype=jnp.float32))
    log_g = jnp.log(gammas)
    q = q.reshape(b, n, c, h, dk)
    k = k.reshape(b, n, c, h, dk)
    v = v.reshape(b, n, c, h, dv)
    idx = jnp.arange(c)
    diff = idx[:, None] - idx[None, :]
    decay = jnp.where(diff[None] >= 0, jnp.exp(diff[None].astype(jnp.float32) * log_g[:, None, None]), 0.0)
    s = jnp.einsum('bnihd,bnjhd->bnhij', q, k) * decay
    intra = jnp.einsum('bnhij,bnjhe->bnihe', s, v)
    zeta = jnp.exp((c - 1 - idx).astype(jnp.float32)[:, None] * log_g[None, :])
    xi = jnp.exp((idx + 1).astype(jnp.float32)[:, None] * log_g[None, :])
    u = jnp.einsum('bnjhd,jh,bnjhe->nbhde', k, zeta, v)
    g_chunk = jnp.exp(c * log_g)[None, :, None, None]

    def step(state, inc):
        return g_chunk * state + inc, state

    _, s_prev = lax.scan(step, jnp.zeros((b, h, dk, dv), u.dtype), u)
    cross = jnp.einsum('bnihd,ih,nbhde->bnihe', q, xi, s_prev)
    return (intra + cross).reshape(b, t, h, dv).astype(v.dtype)


def even_mixer(h, positions, w_in, w_out):
    b, t, _ = h.shape
    (aq, ak, av, ag, iq, ik, iw, bq, bk, bv, bg) = split_columns(h @ w_in, EVEN_WIDTHS)
    a = dsa_attention(aq.reshape(b, t, A_HEADS, A_HEAD_DIM), ak, av,
                      iq.reshape(b, t, IDX_HEADS, IDX_DIM), ik, iw)
    a = a * jax.nn.silu(ag)
    q = rotary(bq.reshape(b, t, B_HEADS, B_QK_DIM), positions)
    k = rotary(bk.reshape(b, t, B_HEADS, B_QK_DIM), positions) * (B_QK_DIM ** -0.5)
    r = head_group_norm(retention(q, k, bv.reshape(b, t, B_HEADS, B_V_DIM)))
    r = r.reshape(b, t, B_WIDTH) * jax.nn.silu(bg)
    return jnp.concatenate([a, r], axis=-1) @ w_out


def odd_mixer(h, positions, w_in, w_out, lam_params, subln, q_norm, kv_norm, w_uq, w_ukv, layer):
    b, t, _ = h.shape
    (cq, ck, cv, cg, dcq, dckv, dkr, dg) = split_columns(h @ w_in, ODD_WIDTHS)
    q = cq.reshape(b, t, C_HEADS, 2 * C_QK_DIM)
    k = ck.reshape(b, t, C_HEADS, 2 * C_QK_DIM)
    v = cv.reshape(b, t, C_HEADS, C_V_DIM)
    qs = jnp.concatenate([q[..., :C_QK_DIM], q[..., C_QK_DIM:]], axis=2)
    ks = jnp.concatenate([k[..., :C_QK_DIM], k[..., C_QK_DIM:]], axis=2)
    o = causal_attention(qs, ks, jnp.concatenate([v, v], axis=2), C_QK_DIM ** -0.5)
    lam_init = 0.8 - 0.6 * math.exp(-0.3 * layer)
    lam = (jnp.exp(jnp.sum(lam_params[0] * lam_params[1]))
           - jnp.exp(jnp.sum(lam_params[2] * lam_params[3])) + lam_init)
    c = rms_norm(o[:, :, :C_HEADS] - lam * o[:, :, C_HEADS:], subln) * (1.0 - lam_init)
    c = c.reshape(b, t, C_WIDTH) * jax.nn.silu(cg)
    qd = (rms_norm(dcq, q_norm) @ w_uq).reshape(b, t, D_HEADS, D_NOPE_DIM + D_ROPE_DIM)
    kv = (rms_norm(dckv, kv_norm) @ w_ukv).reshape(b, t, D_HEADS, D_NOPE_DIM + D_V_DIM)
    q_full = jnp.concatenate([qd[..., :D_NOPE_DIM], rotary(qd[..., D_NOPE_DIM:], positions)], axis=-1)
    k_rope = jnp.broadcast_to(rotary(dkr[:, :, None, :], positions), (b, t, D_HEADS, D_ROPE_DIM))
    k_full = jnp.concatenate([kv[..., :D_NOPE_DIM], k_rope], axis=-1)
    m = causal_attention(q_full, k_full, kv[..., D_NOPE_DIM:], (D_NOPE_DIM + D_ROPE_DIM) ** -0.5)
    m = m.reshape(b, t, D_WIDTH) * jax.nn.silu(dg)
    return jnp.concatenate([c, m], axis=-1) @ w_out


def setup_inputs(seed: int = 0) -> dict:
    key = jax.random.key(seed)
    ks = jax.random.split(key, 16)
    n_even = (DEPTH + 1) // 2
    n_odd = DEPTH // 2
    nrm = jax.random.normal
    f32 = jnp.float32
    x = nrm(ks[0], (BATCH, SEQ, D_MODEL), f32)
    offset = jax.random.randint(ks[1], (BATCH, 1), 0, MAX_POS_OFFSET)
    positions = (offset + jnp.arange(SEQ)[None, :]).astype(jnp.int32)
    return {
        'x': x,
        'positions': positions,
        'pre_norm': 1.0 + 0.02 * nrm(ks[2], (DEPTH, D_MODEL), f32),
        'post_norm': 1.0 + 0.02 * nrm(ks[3], (DEPTH, D_MODEL), f32),
        'w_in_even': nrm(ks[4], (n_even, D_MODEL, EVEN_IN), f32) * D_MODEL ** -0.5,
        'w_out_even': nrm(ks[5], (n_even, EVEN_MIX_WIDTH, D_MODEL), f32) * EVEN_MIX_WIDTH ** -0.5,
        'w_in_odd': nrm(ks[6], (n_odd, D_MODEL, ODD_IN), f32) * D_MODEL ** -0.5,
        'diff_lambda': 0.1 * nrm(ks[7], (n_odd, 4, C_QK_DIM), f32),
        'diff_subln': 1.0 + 0.02 * nrm(ks[8], (n_odd, C_V_DIM), f32),
        'mla_q_norm': 1.0 + 0.02 * nrm(ks[9], (n_odd, D_Q_LORA), f32),
        'mla_kv_norm': 1.0 + 0.02 * nrm(ks[10], (n_odd, D_KV_LORA), f32),
        'mla_w_uq': nrm(ks[11], (n_odd, D_Q_LORA, D_HEADS * (D_NOPE_DIM + D_ROPE_DIM)), f32) * D_Q_LORA ** -0.5,
        'mla_w_ukv': nrm(ks[12], (n_odd, D_KV_LORA, D_HEADS * (D_NOPE_DIM + D_V_DIM)), f32) * D_KV_LORA ** -0.5,
        'w_out_odd': nrm(ks[13], (n_odd, ODD_MIX_WIDTH, D_MODEL), f32) * ODD_MIX_WIDTH ** -0.5,
    }


def reference(x, positions, pre_norm, post_norm, w_in_even, w_out_even, w_in_odd,
              diff_lambda, diff_subln, mla_q_norm, mla_kv_norm, mla_w_uq, mla_w_ukv, w_out_odd):
    for layer in range(DEPTH):
        h = rms_norm(x, pre_norm[layer])
        j = layer // 2
        if layer % 2 == 0:
            m = even_mixer(h, positions, w_in_even[j], w_out_even[j])
        else:
            m = odd_mixer(h, positions, w_in_odd[j], w_out_odd[j], diff_lambda[j], diff_subln[j],
                          mla_q_norm[j], mla_kv_norm[j], mla_w_uq[j], mla_w_ukv[j], layer)
        x = x + rms_norm(m, post_norm[layer])
    return x
```

```python
import functools
import math

import jax
import jax.numpy as jnp
import numpy as np
from jax import lax
from jax.experimental import pallas as pl
from jax.experimental.pallas import tpu as pltpu

F32 = jnp.float32
BF16 = jnp.bfloat16

D_MODEL = 1024
NORM_EPS = 1e-6
ROPE_BASE = 10000.0

A_HEADS, A_HEAD_DIM = 8, 64
IDX_HEADS, IDX_DIM = 4, 64
DSA_TOPK = 256
B_HEADS, B_QK_DIM, B_V_DIM = 4, 64, 128
RET_CHUNK = 128
C_HEADS, C_QK_DIM, C_V_DIM = 4, 64, 128
D_HEADS, D_NOPE_DIM, D_ROPE_DIM, D_V_DIM = 8, 64, 32, 64
D_Q_LORA, D_KV_LORA = 256, 128

LANES = 128
NEG_BIG = -1e30
INT_MIN = -(2 ** 31)
VMEM_LIMIT = 56 * 1024 * 1024

PROJ_ROWS = 512
SEL_Q = 128
SEL_K = 512
FLASH_T = 512


def _silu(x):
    return x * (1.0 / (1.0 + jnp.exp(-x)))


def _dot(a, b):
    return jnp.dot(a, b, preferred_element_type=F32)


def _dot_nt(a, b):
    return lax.dot_general(a, b, (((1,), (1,)), ((), ())), preferred_element_type=F32)


def _dot_tn(a, b):
    return lax.dot_general(a, b, (((0,), (0,)), ((), ())), preferred_element_type=F32)


def _split_bf16(x):
    hi = x.astype(BF16)
    lo = (x - hi.astype(F32)).astype(BF16)
    return hi, lo


def _inproj_body(x_ref, g_ref, w_ref, whi_ref, wlo_ref, o_ref, oaux_ref):
    x = x_ref[...]
    h = x * lax.rsqrt(jnp.mean(x * x, axis=-1, keepdims=True) + NORM_EPS) * g_ref[...]
    hb, hl = _split_bf16(h)
    o_ref[...] = _dot(hb, w_ref[...]).astype(o_ref.dtype)
    whi = whi_ref[...]
    oaux_ref[...] = _dot(hb, whi) + _dot(hl, whi) + _dot(hb, wlo_ref[...])


def _inproj(x2, gain, w_main, w_aux):
    n = x2.shape[0]
    cm, ca = w_main.shape[1], w_aux.shape[1]
    whi, wlo = _split_bf16(w_aux)
    return pl.pallas_call(
        _inproj_body,
        out_shape=(jax.ShapeDtypeStruct((n, cm), BF16), jax.ShapeDtypeStruct((n, ca), F32)),
        grid=(n // PROJ_ROWS,),
        in_specs=[
            pl.BlockSpec((PROJ_ROWS, D_MODEL), lambda i: (i, 0)),
            pl.BlockSpec((1, D_MODEL), lambda i: (0, 0)),
            pl.BlockSpec((D_MODEL, cm), lambda i: (0, 0)),
            pl.BlockSpec((D_MODEL, ca), lambda i: (0, 0)),
            pl.BlockSpec((D_MODEL, ca), lambda i: (0, 0)),
        ],
        out_specs=(pl.BlockSpec((PROJ_ROWS, cm), lambda i: (i, 0)),
                   pl.BlockSpec((PROJ_ROWS, ca), lambda i: (i, 0))),
        compiler_params=pltpu.CompilerParams(
            dimension_semantics=("arbitrary",), vmem_limit_bytes=VMEM_LIMIT),
        name="inproj",
    )(x2, gain.reshape(1, D_MODEL), w_main.astype(BF16), whi, wlo)


def _sortable_key(score):
    score = jnp.where(score == 0.0, 0.0, score)
    bits = pltpu.bitcast(score, jnp.int32)
    return bits ^ ((bits >> 31) & jnp.int32(0x7FFFFFFF))


def _lane_fold(x):
    acc = x[:, :LANES]
    for c in range(1, x.shape[1] // LANES):
        acc = acc + x[:, c * LANES:(c + 1) * LANES]
    return acc


def _dsa_select_body(aux_q_ref, aux_k_ref, tri_ref, bias_ref, keys_ref, *, topk, seq):
    i = pl.program_id(1)
    q0 = i * SEL_Q
    nkt = (q0 + SEL_Q + SEL_K - 1) // SEL_K
    n_all = seq // SEL_K

    aq = aux_q_ref[...]
    w = aq[:, 384:512] * (IDX_HEADS ** -0.5 * IDX_DIM ** -0.5)
    q_ops = []
    for h in range(IDX_HEADS):
        qh = aq[:, h * IDX_DIM:(h + 1) * IDX_DIM]
        hi, lo = _split_bf16(qh)
        q_ops.append(jnp.concatenate([hi, lo, hi], axis=1))
    w_cols = [w[:, h:h + 1] for h in range(IDX_HEADS)]
    qpos = q0 + lax.broadcasted_iota(jnp.int32, (SEL_Q, SEL_K), 0)
    kiota = lax.broadcasted_iota(jnp.int32, (SEL_Q, SEL_K), 1)

    def key_slice(kt):
        return pl.ds(pl.multiple_of(kt * SEL_K, SEL_K), SEL_K)

    def score_tile(kt, carry):
        kk = aux_k_ref[key_slice(kt), 256:256 + IDX_DIM]
        khi, klo = _split_bf16(kk)
        k_op = jnp.concatenate([khi, khi, klo], axis=1)
        score = jnp.zeros((SEL_Q, SEL_K), F32)
        for h in range(IDX_HEADS):
            rel = jnp.maximum(_dot_nt(q_ops[h], k_op), 0.0)
            score = score + w_cols[h] * rel
        key = _sortable_key(score)
        key = jnp.where(kt * SEL_K + kiota <= qpos, key, INT_MIN)
        keys_ref[:, key_slice(kt)] = key
        return carry

    lax.fori_loop(0, nkt, score_tile, 0)

    def count(pred_fn):
        def body(kt, acc):
            tile = keys_ref[:, key_slice(kt)]
            return acc + _lane_fold(jnp.where(pred_fn(tile), 1.0, 0.0))
        acc = lax.fori_loop(0, nkt, body, jnp.zeros((SEL_Q, LANES), F32))
        return jnp.sum(acc, axis=1, keepdims=True)

    def bit_pass(b, thr):
        cand = thr + (jnp.int32(1) << (31 - b))
        cnt = count(lambda t: t >= cand)
        return jnp.where(cnt >= float(topk), cand, thr)

    thr = lax.fori_loop(0, 32, bit_pass, jnp.full((SEL_Q, 1), INT_MIN, jnp.int32))
    n_gt = count(lambda t: t > thr)
    need = jnp.where(thr == INT_MIN, 0.0, float(topk) - n_gt)
    tri = tri_ref[...]

    def emit_tile(kt, seen):
        tile = keys_ref[:, key_slice(kt)]
        tie = tile == thr
        tie_b = jnp.where(tie, 1.0, 0.0).astype(BF16)
        rank = seen + _dot(tie_b, tri)
        sel = (tile > thr) | (tie & (rank <= need))
        bias_ref[:, key_slice(kt)] = jnp.where(sel, 0.0, NEG_BIG).astype(bias_ref.dtype)
        return seen + jnp.sum(jnp.where(tie, 1.0, 0.0), axis=1, keepdims=True)

    lax.fori_loop(0, nkt, emit_tile, jnp.zeros((SEL_Q, 1), F32))

    def fill_tile(kt, carry):
        bias_ref[:, key_slice(kt)] = jnp.full((SEL_Q, SEL_K), NEG_BIG, bias_ref.dtype)
        return carry

    lax.fori_loop(nkt, n_all, fill_tile, 0)


def _dsa_select(aux, topk):
    b, t, ca = aux.shape
    tri = jnp.triu(jnp.ones((SEL_K, SEL_K), F32)).astype(BF16)
    return pl.pallas_call(
        functools.partial(_dsa_select_body, topk=topk, seq=t),
        out_shape=jax.ShapeDtypeStruct((b, t, t), BF16),
        grid=(b, t // SEL_Q),
        in_specs=[
            pl.BlockSpec((None, SEL_Q, ca), lambda bb, i: (bb, i, 0)),
            pl.BlockSpec((None, t, ca), lambda bb, i: (bb, 0, 0)),
            pl.BlockSpec((SEL_K, SEL_K), lambda bb, i: (0, 0)),
        ],
        out_specs=pl.BlockSpec((None, SEL_Q, t), lambda bb, i: (bb, i, 0)),
        scratch_shapes=[pltpu.VMEM((SEL_Q, t), jnp.int32)],
        compiler_params=pltpu.CompilerParams(
            dimension_semantics=("arbitrary", "arbitrary"), vmem_limit_bytes=VMEM_LIMIT),
        name="dsa_select",
    )(aux, aux, tri)


def _flash_body(it_ref, jt_ref, *refs, heads, scale, fold_scale, has_bias, out_mode):
    if has_bias:
        q_ref, k_ref, v_ref, bias_ref, o_ref, qm_ref, m_ref, l_ref, acc_ref = refs
    else:
        q_ref, k_ref, v_ref, o_ref, qm_ref, m_ref, l_ref, acc_ref = refs
        bias_ref = None
    step = pl.program_id(1)
    i = it_ref[step]
    j = jt_ref[step]
    tq = FLASH_T
    lane = lax.broadcasted_iota(jnp.int32, (tq, LANES), 1)

    @pl.when(j == 0)
    def _init():
        m_ref[...] = jnp.full(m_ref.shape, NEG_BIG, F32)
        l_ref[...] = jnp.zeros(l_ref.shape, F32)
        acc_ref[...] = jnp.zeros(acc_ref.shape, F32)
        for vh, (qb, half, _, _) in enumerate(heads):
            q = q_ref[:, qb * LANES:(qb + 1) * LANES]
            if fold_scale:
                q = q * scale
            if half is not None:
                q = jnp.where((lane >= 64 * half) & (lane < 64 * (half + 1)), q, 0.0)
            qm_ref[vh] = q.astype(BF16)

    def tile(diag):
        if has_bias:
            bias = bias_ref[...].astype(F32)
        if diag:
            row = lax.broadcasted_iota(jnp.int32, (tq, FLASH_T), 0)
            col = lax.broadcasted_iota(jnp.int32, (tq, FLASH_T), 1)
            causal = col <= row
        for vh, (_, _, kb, vb) in enumerate(heads):
            s = _dot_nt(qm_ref[vh], k_ref[:, kb * LANES:(kb + 1) * LANES])
            if not fold_scale:
                s = s * scale
            if has_bias:
                s = s + bias
            if diag:
                s = jnp.where(causal, s, NEG_BIG)
            m_prev = m_ref[vh]
            m_new = jnp.maximum(m_prev, jnp.max(s, axis=1, keepdims=True))
            alpha = jnp.exp(m_prev - m_new)
            p = jnp.exp(s - m_new)
            l_ref[vh] = alpha * l_ref[vh] + jnp.sum(p, axis=1, keepdims=True)
            pv = _dot(p.astype(BF16), v_ref[:, vb * LANES:(vb + 1) * LANES])
            acc_ref[vh] = alpha * acc_ref[vh] + pv
            m_ref[vh] = m_new

    if has_bias:
        tile(False)
    else:
        @pl.when(j == i)
        def _diag():
            tile(True)

        @pl.when(j != i)
        def _off():
            tile(False)

    @pl.when(j == i)
    def _finish():
        outs = [acc_ref[vh] * (1.0 / l_ref[vh]) for vh in range(len(heads))]
        if out_mode == "pair":
            for p in range(len(heads) // 2):
                o_ref[:, p * LANES:(p + 1) * LANES] = jnp.where(
                    lane < 64, outs[2 * p], outs[2 * p + 1]).astype(o_ref.dtype)
        else:
            for vh in range(len(heads)):
                o_ref[:, vh * LANES:(vh + 1) * LANES] = outs[vh].astype(o_ref.dtype)


def _flash(q_arr, q_blk, q_w, k_arr, k_blk, k_w, v_arr, v_blk, v_w, heads, scale, out_mode,
           bias=None):
    b, t, _ = q_arr.shape
    nt = t // FLASH_T
    pairs = [(i, j) for i in range(nt) for j in range(i + 1)]
    it = jnp.asarray(np.array([p[0] for p in pairs], np.int32))
    jt = jnp.asarray(np.array([p[1] for p in pairs], np.int32))
    nvh = len(heads)
    out_w = (nvh // 2 if out_mode == "pair" else nvh) * LANES
    fold = float(np.log2(scale)).is_integer()
    in_specs = [
        pl.BlockSpec((None, FLASH_T, q_w), lambda bb, s, it_r, jt_r: (bb, it_r[s], q_blk)),
        pl.BlockSpec((None, FLASH_T, k_w), lambda bb, s, it_r, jt_r: (bb, jt_r[s], k_blk)),
        pl.BlockSpec((None, FLASH_T, v_w), lambda bb, s, it_r, jt_r: (bb, jt_r[s], v_blk)),
    ]
    args = [q_arr, k_arr, v_arr]
    if bias is not None:
        in_specs.append(pl.BlockSpec((None, FLASH_T, FLASH_T),
                                     lambda bb, s, it_r, jt_r: (bb, it_r[s], jt_r[s])))
        args.append(bias)
    return pl.pallas_call(
        functools.partial(_flash_body, heads=tuple(heads), scale=scale, fold_scale=fold,
                          has_bias=bias is not None, out_mode=out_mode),
        out_shape=jax.ShapeDtypeStruct((b, t, out_w), F32),
        grid_spec=pltpu.PrefetchScalarGridSpec(
            num_scalar_prefetch=2,
            grid=(b, len(pairs)),
            in_specs=in_specs,
            out_specs=pl.BlockSpec((None, FLASH_T, out_w),
                                   lambda bb, s, it_r, jt_r: (bb, it_r[s], 0)),
            scratch_shapes=[
                pltpu.VMEM((nvh, FLASH_T, LANES), BF16),
                pltpu.VMEM((nvh, FLASH_T, 1), F32),
                pltpu.VMEM((nvh, FLASH_T, 1), F32),
                pltpu.VMEM((nvh, FLASH_T, LANES), F32),
            ]),
        compiler_params=pltpu.CompilerParams(
            dimension_semantics=("arbitrary", "arbitrary"), vmem_limit_bytes=VMEM_LIMIT),
        name="flash_" + out_mode + ("_bias" if bias is not None else ""),
    )(it, jt, *args)


def _rope_tables(pos_col, invf_row):
    ang = pos_col * invf_row
    return jnp.cos(ang), jnp.sin(ang)


def _rope_apply(x, cos, sin, first_half, half):
    fwd = pltpu.roll(x, LANES - half, 1)
    bwd = pltpu.roll(x, half, 1)
    return x * cos + jnp.where(first_half, -fwd, bwd) * sin


def _retention_body(qk_ref, v_ref, g_ref, pos_ref, invf_ref, decay_ref, zeta_ref, xi_ref,
                    gch_ref, o_ref, state_ref):
    c = RET_CHUNK

    @pl.when(pl.program_id(1) == 0)
    def _init():
        state_ref[...] = jnp.zeros(state_ref.shape, F32)

    lane = lax.broadcasted_iota(jnp.int32, (c, LANES), 1)
    first_half = (lane % B_QK_DIM) < (B_QK_DIM // 2)
    cos, sin = _rope_tables(pos_ref[...], invf_ref[...])
    qk = qk_ref[...].astype(F32)
    nblk = B_HEADS * B_QK_DIM // LANES
    q_blk = [_rope_apply(qk[:, p * LANES:(p + 1) * LANES], cos, sin, first_half, B_QK_DIM // 2)
             for p in range(nblk)]
    k_blk = [_rope_apply(qk[:, (nblk + p) * LANES:(nblk + p + 1) * LANES], cos, sin, first_half,
                         B_QK_DIM // 2) * (B_QK_DIM ** -0.5) for p in range(nblk)]
    for h in range(B_HEADS):
        p, half = divmod(h, 2)
        own = (lane >= 64 * half) & (lane < 64 * (half + 1))
        q = jnp.where(own, q_blk[p], 0.0)
        k = k_blk[p]
        v = v_ref[:, h * B_V_DIM:(h + 1) * B_V_DIM]
        s = _dot_nt(q.astype(BF16), k.astype(BF16)) * decay_ref[h]
        intra = _dot(s.astype(BF16), v)
        st = state_ref[h]
        cross = _dot((q * xi_ref[h]).astype(BF16), st.astype(BF16))
        u = _dot_tn((k * zeta_ref[h]).astype(BF16), v)
        state_ref[h] = gch_ref[h] * st + u
        o = intra + cross
        mu = jnp.mean(o, axis=-1, keepdims=True)
        d = o - mu
        var = jnp.mean(d * d, axis=-1, keepdims=True)
        r = d * lax.rsqrt(var + NORM_EPS)
        gate = g_ref[:, h * B_V_DIM:(h + 1) * B_V_DIM].astype(F32)
        o_ref[:, h * B_V_DIM:(h + 1) * B_V_DIM] = (r * _silu(gate)).astype(o_ref.dtype)


def _retention(main, posf, qk_blk, v_blk, g_blk):
    b, t, _ = main.shape
    c = RET_CHUNK
    half = B_QK_DIM // 2
    inv_freq = ROPE_BASE ** (-jnp.arange(half, dtype=F32) / half)
    invf = jnp.tile(inv_freq, LANES // half).reshape(1, LANES)
    gammas = 1.0 - 2.0 ** (-5.0 - jnp.arange(B_HEADS, dtype=F32))
    log_g = jnp.log(gammas)
    idx = jnp.arange(c)
    diff = idx[:, None] - idx[None, :]
    decay = jnp.where(diff[None] >= 0,
                      jnp.exp(diff[None].astype(F32) * log_g[:, None, None]), 0.0)
    zeta = jnp.exp((c - 1 - idx).astype(F32)[None, :] * log_g[:, None])
    xi = jnp.exp((idx + 1).astype(F32)[None, :] * log_g[:, None])
    zeta = jnp.broadcast_to(zeta[:, :, None], (B_HEADS, c, LANES))
    xi = jnp.broadcast_to(xi[:, :, None], (B_HEADS, c, LANES))
    gch = jnp.broadcast_to(jnp.exp(c * log_g)[:, None, None], (B_HEADS, LANES, LANES))
    w = B_HEADS * B_V_DIM
    const = lambda shape: pl.BlockSpec(shape, lambda bb, n: (0,) * len(shape))
    return pl.pallas_call(
        _retention_body,
        out_shape=jax.ShapeDtypeStruct((b, t, w), BF16),
        grid=(b, t // c),
        in_specs=[
            pl.BlockSpec((None, c, w), lambda bb, n: (bb, n, qk_blk)),
            pl.BlockSpec((None, c, w), lambda bb, n: (bb, n, v_blk)),
            pl.BlockSpec((None, c, w), lambda bb, n: (bb, n, g_blk)),
            pl.BlockSpec((None, c, 1), lambda bb, n: (bb, n, 0)),
            const((1, LANES)),
            const((B_HEADS, c, c)),
            const((B_HEADS, c, LANES)),
            const((B_HEADS, c, LANES)),
            const((B_HEADS, LANES, LANES)),
        ],
        out_specs=pl.BlockSpec((None, c, w), lambda bb, n: (bb, n, 0)),
        scratch_shapes=[pltpu.VMEM((B_HEADS, LANES, B_V_DIM), F32)],
        compiler_params=pltpu.CompilerParams(
            dimension_semantics=("arbitrary", "arbitrary"), vmem_limit_bytes=VMEM_LIMIT),
        name="retention",
    )(main, main, main, posf, invf, decay, zeta, xi, gch)


def _mla_prep_body(aux_ref, pos_ref, invf_ref, qn_ref, kvn_ref, wq_ref, wk_ref, wv_ref,
                   q_ref, k_ref, v_ref):
    rows = aux_ref.shape[0]
    aux = aux_ref[...]
    cq = aux[:, :D_Q_LORA]
    ckv = aux[:, D_Q_LORA:D_Q_LORA + D_KV_LORA]
    kr = aux[:, D_Q_LORA + D_KV_LORA:]
    cq = cq * lax.rsqrt(jnp.mean(cq * cq, axis=-1, keepdims=True) + NORM_EPS) * qn_ref[...]
    ckv = ckv * lax.rsqrt(jnp.mean(ckv * ckv, axis=-1, keepdims=True) + NORM_EPS) * kvn_ref[...]
    cos, sin = _rope_tables(pos_ref[...], invf_ref[...])
    lane = lax.broadcasted_iota(jnp.int32, (rows, LANES), 1)
    half = D_ROPE_DIM // 2
    first_half = lane < D_NOPE_DIM + half
    qf = _dot(cq.astype(BF16), wq_ref[...])
    kf = _dot(ckv.astype(BF16), wk_ref[...])
    kr = _rope_apply(kr, cos, sin, first_half, half)
    for h in range(D_HEADS):
        sl = slice(h * LANES, (h + 1) * LANES)
        q_ref[:, sl] = _rope_apply(qf[:, sl], cos, sin, first_half, half).astype(q_ref.dtype)
        k_ref[:, sl] = (kf[:, sl] + kr).astype(k_ref.dtype)
    v_ref[...] = _dot(ckv.astype(BF16), wv_ref[...]).astype(v_ref.dtype)


def _mla_prep(aux2, posf2, q_norm, kv_norm, w_uq, w_ukv):
    n = aux2.shape[0]
    half = D_ROPE_DIM // 2
    inv_freq = ROPE_BASE ** (-jnp.arange(half, dtype=F32) / half)
    invf = jnp.zeros((LANES,), F32).at[D_NOPE_DIM:D_NOPE_DIM + D_ROPE_DIM].set(jnp.tile(inv_freq, 2))
    dq = D_NOPE_DIM + D_ROPE_DIM
    wq = jnp.pad(w_uq.reshape(D_Q_LORA, D_HEADS, dq), ((0, 0), (0, 0), (0, LANES - dq)))
    wq = wq.reshape(D_Q_LORA, D_HEADS * LANES).astype(BF16)
    wkv = w_ukv.reshape(D_KV_LORA, D_HEADS, D_NOPE_DIM + D_V_DIM)
    wk = jnp.pad(wkv[:, :, :D_NOPE_DIM], ((0, 0), (0, 0), (0, LANES - D_NOPE_DIM)))
    wk = wk.reshape(D_KV_LORA, D_HEADS * LANES).astype(BF16)
    wv = wkv[:, :, D_NOPE_DIM:].reshape(D_KV_LORA, D_HEADS * D_V_DIM).astype(BF16)
    rows = PROJ_ROWS
    const = lambda shape: pl.BlockSpec(shape, lambda i: (0,) * len(shape))
    return pl.pallas_call(
        _mla_prep_body,
        out_shape=(jax.ShapeDtypeStruct((n, D_HEADS * LANES), BF16),
                   jax.ShapeDtypeStruct((n, D_HEADS * LANES), BF16),
                   jax.ShapeDtypeStruct((n, D_HEADS * D_V_DIM), BF16)),
        grid=(n // rows,),
        in_specs=[
            pl.BlockSpec((rows, aux2.shape[1]), lambda i: (i, 0)),
            pl.BlockSpec((rows, 1), lambda i: (i, 0)),
            const((1, LANES)),
            const((1, D_Q_LORA)),
            const((1, D_KV_LORA)),
            const(wq.shape), const(wk.shape), const(wv.shape),
        ],
        out_specs=(pl.BlockSpec((rows, D_HEADS * LANES), lambda i: (i, 0)),
                   pl.BlockSpec((rows, D_HEADS * LANES), lambda i: (i, 0)),
                   pl.BlockSpec((rows, D_HEADS * D_V_DIM), lambda i: (i, 0))),
        compiler_params=pltpu.CompilerParams(
            dimension_semantics=("arbitrary",), vmem_limit_bytes=VMEM_LIMIT),
        name="mla_prep",
    )(aux2, posf2, invf.reshape(1, LANES), q_norm.reshape(1, -1), kv_norm.reshape(1, -1),
      wq, wk, wv)


def _outproj_tail(m_lo, m_hi, w_ref, pg_ref, x_ref, o_ref):
    half = w_ref.shape[0] // 2
    y = _dot(m_lo, w_ref[:half, :]) + _dot(m_hi, w_ref[half:, :])
    y = y * lax.rsqrt(jnp.mean(y * y, axis=-1, keepdims=True) + NORM_EPS) * pg_ref[...]
    o_ref[...] = x_ref[...] + y


def _outproj_even_body(a_ref, ag_ref, r_ref, w_ref, pg_ref, x_ref, o_ref):
    a = a_ref[...] * _silu(ag_ref[...].astype(F32))
    _outproj_tail(a.astype(BF16), r_ref[...], w_ref, pg_ref, x_ref, o_ref)


def _outproj_odd_body(od_ref, cg_ref, om_ref, dg_ref, lam_ref, sub_ref, w_ref, pg_ref, x_ref,
                      o_ref, *, lam_init):
    lp = lam_ref[...]
    lam = (jnp.exp(jnp.sum(lp[0:1] * lp[1:2], axis=-1, keepdims=True))
           - jnp.exp(jnp.sum(lp[2:3] * lp[3:4], axis=-1, keepdims=True)) + lam_init)
    od = od_ref[...]
    cg = cg_ref[...].astype(F32)
    parts = []
    for h in range(C_HEADS):
        d = (od[:, h * LANES:(h + 1) * LANES]
             - lam * od[:, (C_HEADS + h) * LANES:(C_HEADS + h + 1) * LANES])
        d = d * lax.rsqrt(jnp.mean(d * d, axis=-1, keepdims=True) + NORM_EPS) * sub_ref[...]
        d = d * (1.0 - lam_init)
        parts.append((d * _silu(cg[:, h * LANES:(h + 1) * LANES])).astype(BF16))
    m_c = jnp.concatenate(parts, axis=1)
    m_d = (om_ref[...] * _silu(dg_ref[...].astype(F32))).astype(BF16)
    _outproj_tail(m_c, m_d, w_ref, pg_ref, x_ref, o_ref)


def _outproj_even(a2, main2, ag_blk, r2, w_out, post_gain, x2):
    n = x2.shape[0]
    rows = PROJ_ROWS
    w = a2.shape[1]
    return pl.pallas_call(
        _outproj_even_body,
        out_shape=jax.ShapeDtypeStruct((n, D_MODEL), F32),
        grid=(n // rows,),
        in_specs=[
            pl.BlockSpec((rows, w), lambda i: (i, 0)),
            pl.BlockSpec((rows, w), lambda i: (i, ag_blk)),
            pl.BlockSpec((rows, w), lambda i: (i, 0)),
            pl.BlockSpec((2 * w, D_MODEL), lambda i: (0, 0)),
            pl.BlockSpec((1, D_MODEL), lambda i: (0, 0)),
            pl.BlockSpec((rows, D_MODEL), lambda i: (i, 0)),
        ],
        out_specs=pl.BlockSpec((rows, D_MODEL), lambda i: (i, 0)),
        compiler_params=pltpu.CompilerParams(
            dimension_semantics=("arbitrary",), vmem_limit_bytes=VMEM_LIMIT),
        name="outproj_even",
    )(a2, main2, r2, w_out.astype(BF16), post_gain.reshape(1, D_MODEL), x2)


def _outproj_odd(od2, main2, cg_blk, om2, dg_blk, lam_params, subln, w_out, post_gain, x2,
                 lam_init):
    n = x2.shape[0]
    rows = PROJ_ROWS
    w = om2.shape[1]
    return pl.pallas_call(
        functools.partial(_outproj_odd_body, lam_init=lam_init),
        out_shape=jax.ShapeDtypeStruct((n, D_MODEL), F32),
        grid=(n // rows,),
        in_specs=[
            pl.BlockSpec((rows, od2.shape[1]), lambda i: (i, 0)),
            pl.BlockSpec((rows, w), lambda i: (i, cg_blk)),
            pl.BlockSpec((rows, w), lambda i: (i, 0)),
            pl.BlockSpec((rows, w), lambda i: (i, dg_blk)),
            pl.BlockSpec(lam_params.shape, lambda i: (0, 0)),
            pl.BlockSpec((1, C_V_DIM), lambda i: (0, 0)),
            pl.BlockSpec((2 * w, D_MODEL), lambda i: (0, 0)),
            pl.BlockSpec((1, D_MODEL), lambda i: (0, 0)),
            pl.BlockSpec((rows, D_MODEL), lambda i: (i, 0)),
        ],
        out_specs=pl.BlockSpec((rows, D_MODEL), lambda i: (i, 0)),
        compiler_params=pltpu.CompilerParams(
            dimension_semantics=("arbitrary",), vmem_limit_bytes=VMEM_LIMIT),
        name="outproj_odd",
    )(od2, main2, om2, main2, lam_params, subln.reshape(1, C_V_DIM), w_out.astype(BF16),
      post_gain.reshape(1, D_MODEL), x2)


def _cols(w, start, width):
    return w[:, start:start + width]


def _even_layer(x, posf, pre_gain, post_gain, w_in, w_out):
    b, t, _ = x.shape
    n = b * t
    aw = A_HEADS * A_HEAD_DIM
    iqw = IDX_HEADS * IDX_DIM
    o = 0
    aq, o = _cols(w_in, o, aw), o + aw
    ak, o = _cols(w_in, o, aw), o + aw
    av, o = _cols(w_in, o, aw), o + aw
    ag, o = _cols(w_in, o, aw), o + aw
    iq, o = _cols(w_in, o, iqw), o + iqw
    ik, o = _cols(w_in, o, IDX_DIM), o + IDX_DIM
    iw, o = _cols(w_in, o, IDX_HEADS), o + IDX_HEADS
    bq, o = _cols(w_in, o, B_HEADS * B_QK_DIM), o + B_HEADS * B_QK_DIM
    bk, o = _cols(w_in, o, B_HEADS * B_QK_DIM), o + B_HEADS * B_QK_DIM
    bv, o = _cols(w_in, o, B_HEADS * B_V_DIM), o + B_HEADS * B_V_DIM
    bg, o = _cols(w_in, o, B_HEADS * B_V_DIM), o + B_HEADS * B_V_DIM
    w_main = jnp.concatenate([aq, ak, av, ag, bq, bk, bv, bg], axis=1)
    zeros = lambda c: jnp.zeros((D_MODEL, c), F32)
    w_aux = jnp.concatenate([iq, ik, zeros(64), iw, zeros(LANES - IDX_HEADS)], axis=1)
    x2 = x.reshape(n, D_MODEL)
    main2, aux2 = _inproj(x2, pre_gain, w_main, w_aux)
    main = main2.reshape(b, t, -1)
    aux = aux2.reshape(b, t, -1)

    topk = min(DSA_TOPK, t // 4)
    bias = _dsa_select(aux, topk)
    heads = [(h // 2, h % 2, h // 2, h // 2) for h in range(A_HEADS)]
    a = _flash(main, 0, aw, main, 1, aw, main, 2, aw, heads, A_HEAD_DIM ** -0.5, "pair",
               bias=bias)
    r = _retention(main, posf, 4, 5, 6)
    return _outproj_even(a.reshape(n, aw), main2, 3, r.reshape(n, -1), w_out, post_gain,
                         x2).reshape(b, t, D_MODEL)


def _odd_layer(x, posf, pre_gain, post_gain, w_in, w_out, lam_params, subln, q_norm, kv_norm,
               w_uq, w_ukv, layer):
    b, t, _ = x.shape
    n = b * t
    cw = C_HEADS * C_V_DIM
    o = 0
    cq, o = _cols(w_in, o, cw), o + cw
    ck, o = _cols(w_in, o, cw), o + cw
    cv, o = _cols(w_in, o, cw), o + cw
    cg, o = _cols(w_in, o, cw), o + cw
    dcq, o = _cols(w_in, o, D_Q_LORA), o + D_Q_LORA
    dckv, o = _cols(w_in, o, D_KV_LORA), o + D_KV_LORA
    dkr, o = _cols(w_in, o, D_ROPE_DIM), o + D_ROPE_DIM
    dg, o = _cols(w_in, o, cw), o + cw
    w_main = jnp.concatenate([cq, ck, cv, cg, dg], axis=1)
    zeros = lambda c: jnp.zeros((D_MODEL, c), F32)
    w_aux = jnp.concatenate([dcq, dckv, zeros(D_NOPE_DIM), dkr,
                             zeros(LANES - D_NOPE_DIM - D_ROPE_DIM)], axis=1)
    x2 = x.reshape(n, D_MODEL)
    main2, aux2 = _inproj(x2, pre_gain, w_main, w_aux)
    main = main2.reshape(b, t, -1)

    heads_c = [(h, m, h, h) for m in range(2) for h in range(C_HEADS)]
    od = _flash(main, 0, cw, main, 1, cw, main, 2, cw, heads_c, C_QK_DIM ** -0.5, "raw")

    qd, kd, vd = _mla_prep(aux2, posf.reshape(n, 1), q_norm, kv_norm, w_uq, w_ukv)
    heads_d = [(h, None, h, h // 2) for h in range(D_HEADS)]
    om = _flash(qd.reshape(b, t, -1), 0, D_HEADS * LANES, kd.reshape(b, t, -1), 0,
                D_HEADS * LANES, vd.reshape(b, t, -1), 0, D_HEADS * D_V_DIM, heads_d,
                (D_NOPE_DIM + D_ROPE_DIM) ** -0.5, "pair")

    lam_init = 0.8 - 0.6 * math.exp(-0.3 * layer)
    return _outproj_odd(od.reshape(n, -1), main2, 3, om.reshape(n, -1), 4, lam_params, subln,
                        w_out, post_gain, x2, lam_init).reshape(b, t, D_MODEL)


def kernel(x, positions, pre_norm, post_norm, w_in_even, w_out_even, w_in_odd, diff_lambda,
           diff_subln, mla_q_norm, mla_kv_norm, mla_w_uq, mla_w_ukv, w_out_odd):
    b, t, _ = x.shape
    posf = positions.astype(F32).reshape(b, t, 1)
    depth = pre_norm.shape[0]
    for layer in range(depth):
        j = layer // 2
        if layer % 2 == 0:
            x = _even_layer(x, posf, pre_norm[layer], post_norm[layer], w_in_even[j],
                            w_out_even[j])
        else:
            x = _odd_layer(x, posf, pre_norm[layer], post_norm[layer], w_in_odd[j],
                           w_out_odd[j], diff_lambda[j], diff_subln[j], mla_q_norm[j],
                           mla_kv_norm[j], mla_w_uq[j], mla_w_ukv[j], layer)
    return x
```

```python
import functools
import math

import jax
import jax.numpy as jnp
import numpy as np
from jax import lax
from jax.experimental import pallas as pl
from jax.experimental.pallas import tpu as pltpu

F32 = jnp.float32
BF16 = jnp.bfloat16

D_MODEL = 1024
NORM_EPS = 1e-6
ROPE_BASE = 10000.0

A_HEADS, A_HEAD_DIM = 8, 64
IDX_HEADS, IDX_DIM = 4, 64
DSA_TOPK = 256
B_HEADS, B_QK_DIM, B_V_DIM = 4, 64, 128
RET_CHUNK = 128
C_HEADS, C_QK_DIM, C_V_DIM = 4, 64, 128
D_HEADS, D_NOPE_DIM, D_ROPE_DIM, D_V_DIM = 8, 64, 32, 64
D_Q_LORA, D_KV_LORA = 256, 128

LANES = 128
NEG_BIG = -1e30
INT_MIN = -(2 ** 31)
VMEM_LIMIT = 56 * 1024 * 1024

PROJ_ROWS = 512
SEL_Q = 256
SEL_K = 256
FLASH_T = 512


def _silu(x):
    return x * (1.0 / (1.0 + jnp.exp(-x)))


def _dot(a, b):
    return jnp.dot(a, b, preferred_element_type=F32)


def _dot_nt(a, b):
    return lax.dot_general(a, b, (((1,), (1,)), ((), ())), preferred_element_type=F32)


def _dot_tn(a, b):
    return lax.dot_general(a, b, (((0,), (0,)), ((), ())), preferred_element_type=F32)


def _split_bf16(x):
    hi = x.astype(BF16)
    lo = (x - hi.astype(F32)).astype(BF16)
    return hi, lo


def _inproj_body(x_ref, g_ref, w_ref, whi_ref, wlo_ref, o_ref, oaux_ref):
    x = x_ref[...]
    h = x * lax.rsqrt(jnp.mean(x * x, axis=-1, keepdims=True) + NORM_EPS) * g_ref[...]
    hb, hl = _split_bf16(h)
    o_ref[...] = _dot(hb, w_ref[...]).astype(o_ref.dtype)
    whi = whi_ref[...]
    oaux_ref[...] = _dot(hb, whi) + _dot(hl, whi) + _dot(hb, wlo_ref[...])


def _inproj(x2, gain, w_main, w_aux):
    n = x2.shape[0]
    cm, ca = w_main.shape[1], w_aux.shape[1]
    whi, wlo = _split_bf16(w_aux)
    return pl.pallas_call(
        _inproj_body,
        out_shape=(jax.ShapeDtypeStruct((n, cm), BF16), jax.ShapeDtypeStruct((n, ca), F32)),
        grid=(n // PROJ_ROWS,),
        in_specs=[
            pl.BlockSpec((PROJ_ROWS, D_MODEL), lambda i: (i, 0)),
            pl.BlockSpec((1, D_MODEL), lambda i: (0, 0)),
            pl.BlockSpec((D_MODEL, cm), lambda i: (0, 0)),
            pl.BlockSpec((D_MODEL, ca), lambda i: (0, 0)),
            pl.BlockSpec((D_MODEL, ca), lambda i: (0, 0)),
        ],
        out_specs=(pl.BlockSpec((PROJ_ROWS, cm), lambda i: (i, 0)),
                   pl.BlockSpec((PROJ_ROWS, ca), lambda i: (i, 0))),
        compiler_params=pltpu.CompilerParams(
            dimension_semantics=("arbitrary",), vmem_limit_bytes=VMEM_LIMIT),
        name="inproj",
    )(x2, gain.reshape(1, D_MODEL), w_main.astype(BF16), whi, wlo)


def _sortable_key(score):
    score = jnp.where(score == 0.0, 0.0, score)
    bits = pltpu.bitcast(score, jnp.int32)
    return bits ^ ((bits >> 31) & jnp.int32(0x7FFFFFFF))


def _row_fold(x):
    return jnp.sum(x.reshape(x.shape[0] // 8, 8, x.shape[1]), axis=0)


def _dsa_select_body(aux_q_ref, aux_k_ref, tril_ref, bias_ref, keys_ref, *, topk, seq):
    i = pl.program_id(1)
    q0 = i * SEL_Q
    nkt = (q0 + SEL_Q + SEL_K - 1) // SEL_K
    n_all = seq // SEL_K

    aq = aux_q_ref[...]
    w_t = (aq[:, 384:512] * (IDX_HEADS ** -0.5 * IDX_DIM ** -0.5)).T
    w_rows = [w_t[h:h + 1, :] for h in range(IDX_HEADS)]
    q_ops = []
    for h in range(IDX_HEADS):
        qh = aq[:, h * IDX_DIM:(h + 1) * IDX_DIM]
        hi, lo = _split_bf16(qh)
        q_ops.append(jnp.concatenate([hi, lo, hi], axis=1))
    qpos = q0 + lax.broadcasted_iota(jnp.int32, (SEL_K, SEL_Q), 1)
    kiota = lax.broadcasted_iota(jnp.int32, (SEL_K, SEL_Q), 0)

    def key_rows(kt):
        return pl.ds(pl.multiple_of(kt * SEL_K, SEL_K), SEL_K)

    def score_tile(kt, carry):
        kk = aux_k_ref[key_rows(kt), 0:IDX_DIM]
        khi, klo = _split_bf16(kk)
        k_op = jnp.concatenate([khi, khi, klo], axis=1)
        score = jnp.zeros((SEL_K, SEL_Q), F32)
        for h in range(IDX_HEADS):
            rel = jnp.maximum(_dot_nt(k_op, q_ops[h]), 0.0)
            score = score + w_rows[h] * rel
        key = _sortable_key(score)
        key = jnp.where(kt * SEL_K + kiota <= qpos, key, INT_MIN)
        keys_ref[key_rows(kt), :] = key
        return carry

    lax.fori_loop(0, nkt, score_tile, 0)

    def count(pred_fn):
        def body(kt, acc):
            tile = keys_ref[key_rows(kt), :]
            return acc + _row_fold(jnp.where(pred_fn(tile), 1.0, 0.0))
        acc = lax.fori_loop(0, nkt, body, jnp.zeros((8, SEL_Q), F32))
        return jnp.sum(acc, axis=0, keepdims=True)

    def bit_pass(b, thr):
        cand = thr + (jnp.int32(1) << (31 - b))
        cnt = count(lambda t: t >= cand)
        return jnp.where(cnt >= float(topk), cand, thr)

    thr = lax.fori_loop(0, 32, bit_pass, jnp.full((1, SEL_Q), INT_MIN, jnp.int32))
    n_gt = count(lambda t: t > thr)
    need = jnp.where(thr == INT_MIN, 0.0, float(topk) - n_gt)
    tril = tril_ref[...]

    def emit_tile(kt, seen):
        tile = keys_ref[key_rows(kt), :]
        tie = tile == thr
        tie_f = jnp.where(tie, 1.0, 0.0)
        rank = seen + _dot(tril, tie_f.astype(BF16))
        sel = (tile > thr) | (tie & (rank <= need))
        bias_ref[key_rows(kt), :] = jnp.where(sel, 0.0, NEG_BIG).astype(bias_ref.dtype)
        return seen + jnp.sum(tie_f, axis=0, keepdims=True)

    lax.fori_loop(0, nkt, emit_tile, jnp.zeros((1, SEL_Q), F32))

    def fill_tile(kt, carry):
        bias_ref[key_rows(kt), :] = jnp.full((SEL_K, SEL_Q), NEG_BIG, bias_ref.dtype)
        return carry

    lax.fori_loop(nkt, n_all, fill_tile, 0)


def _dsa_select(aux, topk):
    b, t, ca = aux.shape
    tril = jnp.tril(jnp.ones((SEL_K, SEL_K), F32)).astype(BF16)
    return pl.pallas_call(
        functools.partial(_dsa_select_body, topk=topk, seq=t),
        out_shape=jax.ShapeDtypeStruct((b, t, t), BF16),
        grid=(b, t // SEL_Q),
        in_specs=[
            pl.BlockSpec((None, SEL_Q, ca), lambda bb, i: (bb, i, 0)),
            pl.BlockSpec((None, t, LANES), lambda bb, i: (bb, 0, 2)),
            pl.BlockSpec((SEL_K, SEL_K), lambda bb, i: (0, 0)),
        ],
        out_specs=pl.BlockSpec((None, t, SEL_Q), lambda bb, i: (bb, 0, i)),
        scratch_shapes=[pltpu.VMEM((t, SEL_Q), jnp.int32)],
        compiler_params=pltpu.CompilerParams(
            dimension_semantics=("arbitrary", "arbitrary"), vmem_limit_bytes=VMEM_LIMIT),
        name="dsa_select",
    )(aux, aux, tril)


def _flash_body(it_ref, jt_ref, *refs, heads, scale, fold_scale, has_bias):
    if has_bias:
        q_ref, k_ref, vt_ref, bias_ref, o_ref, qm_ref, m_ref, l_ref, acc_ref = refs
    else:
        q_ref, k_ref, vt_ref, o_ref, qm_ref, m_ref, l_ref, acc_ref = refs
        bias_ref = None
    step = pl.program_id(1)
    i = it_ref[step]
    j = jt_ref[step]
    lane = lax.broadcasted_iota(jnp.int32, (FLASH_T, LANES), 1)

    @pl.when(j == 0)
    def _init():
        m_ref[...] = jnp.full(m_ref.shape, NEG_BIG, F32)
        l_ref[...] = jnp.zeros(l_ref.shape, F32)
        acc_ref[...] = jnp.zeros(acc_ref.shape, F32)
        for vh, (qb, half, _, _, _, _) in enumerate(heads):
            q = q_ref[:, qb * LANES:(qb + 1) * LANES]
            if fold_scale:
                q = q * scale
            if half is not None:
                q = jnp.where((lane >= 64 * half) & (lane < 64 * (half + 1)), q, 0.0)
            qm_ref[vh] = q.astype(BF16)

    def tile(diag):
        if has_bias:
            bias = bias_ref[...].astype(F32)
        if diag:
            krow = lax.broadcasted_iota(jnp.int32, (FLASH_T, FLASH_T), 0)
            qcol = lax.broadcasted_iota(jnp.int32, (FLASH_T, FLASH_T), 1)
            causal = krow <= qcol
        for vh, (_, _, kb, v0, nv, o0) in enumerate(heads):
            s = _dot_nt(k_ref[:, kb * LANES:(kb + 1) * LANES], qm_ref[vh])
            if not fold_scale:
                s = s * scale
            if has_bias:
                s = s + bias
            if diag:
                s = jnp.where(causal, s, NEG_BIG)
            m_prev = m_ref[vh]
            m_new = jnp.maximum(m_prev, jnp.max(s, axis=0, keepdims=True))
            alpha = jnp.exp(m_prev - m_new)
            p = jnp.exp(s - m_new)
            l_ref[vh] = alpha * l_ref[vh] + jnp.sum(p, axis=0, keepdims=True)
            pv = _dot(vt_ref[v0:v0 + nv, :], p.astype(BF16))
            acc_ref[o0:o0 + nv, :] = alpha * acc_ref[o0:o0 + nv, :] + pv
            m_ref[vh] = m_new

    if has_bias:
        tile(False)
    else:
        @pl.when(j == i)
        def _diag():
            tile(True)

        @pl.when(j != i)
        def _off():
            tile(False)

    @pl.when(j == i)
    def _finish():
        for vh, (_, _, _, _, nv, o0) in enumerate(heads):
            acc_ref[o0:o0 + nv, :] = acc_ref[o0:o0 + nv, :] * (1.0 / l_ref[vh])
        o_ref[...] = acc_ref[...].T.astype(o_ref.dtype)


def _flash(q_arr, q_blk, q_w, k_arr, k_blk, k_w, vt_arr, heads, scale, bias=None):
    b, t, _ = q_arr.shape
    nt = t // FLASH_T
    pairs = [(i, j) for i in range(nt) for j in range(i + 1)]
    it = jnp.asarray(np.array([p[0] for p in pairs], np.int32))
    jt = jnp.asarray(np.array([p[1] for p in pairs], np.int32))
    nvh = len(heads)
    out_w = max(h[5] + h[4] for h in heads)
    vt_rows = vt_arr.shape[1]
    fold = float(np.log2(scale)).is_integer()
    in_specs = [
        pl.BlockSpec((None, FLASH_T, q_w), lambda bb, s, it_r, jt_r: (bb, it_r[s], q_blk)),
        pl.BlockSpec((None, FLASH_T, k_w), lambda bb, s, it_r, jt_r: (bb, jt_r[s], k_blk)),
        pl.BlockSpec((None, vt_rows, FLASH_T), lambda bb, s, it_r, jt_r: (bb, 0, jt_r[s])),
    ]
    args = [q_arr, k_arr, vt_arr]
    if bias is not None:
        in_specs.append(pl.BlockSpec((None, FLASH_T, FLASH_T),
                                     lambda bb, s, it_r, jt_r: (bb, jt_r[s], it_r[s])))
        args.append(bias)
    return pl.pallas_call(
        functools.partial(_flash_body, heads=tuple(heads), scale=scale, fold_scale=fold,
                          has_bias=bias is not None),
        out_shape=jax.ShapeDtypeStruct((b, t, out_w), F32),
        grid_spec=pltpu.PrefetchScalarGridSpec(
            num_scalar_prefetch=2,
            grid=(b, len(pairs)),
            in_specs=in_specs,
            out_specs=pl.BlockSpec((None, FLASH_T, out_w),
                                   lambda bb, s, it_r, jt_r: (bb, it_r[s], 0)),
            scratch_shapes=[
                pltpu.VMEM((nvh, FLASH_T, LANES), BF16),
                pltpu.VMEM((nvh, 1, FLASH_T), F32),
                pltpu.VMEM((nvh, 1, FLASH_T), F32),
                pltpu.VMEM((out_w, FLASH_T), F32),
            ]),
        compiler_params=pltpu.CompilerParams(
            dimension_semantics=("arbitrary", "arbitrary"), vmem_limit_bytes=VMEM_LIMIT),
        name="flash_%d" % out_w + ("_bias" if bias is not None else ""),
    )(it, jt, *args)


def _rope_tables(pos_col, invf_row):
    ang = pos_col * invf_row
    return jnp.cos(ang), jnp.sin(ang)


def _rope_apply(x, cos, sin, first_half, half):
    fwd = pltpu.roll(x, LANES - half, 1)
    bwd = pltpu.roll(x, half, 1)
    return x * cos + jnp.where(first_half, -fwd, bwd) * sin


def _retention_body(qk_ref, v_ref, g_ref, pos_ref, invf_ref, decay_ref, zeta_ref, xi_ref,
                    gch_ref, o_ref, state_ref):
    c = RET_CHUNK

    @pl.when(pl.program_id(1) == 0)
    def _init():
        state_ref[...] = jnp.zeros(state_ref.shape, F32)

    lane = lax.broadcasted_iota(jnp.int32, (c, LANES), 1)
    first_half = (lane % B_QK_DIM) < (B_QK_DIM // 2)
    cos, sin = _rope_tables(pos_ref[...], invf_ref[...])
    qk = qk_ref[...].astype(F32)
    nblk = B_HEADS * B_QK_DIM // LANES
    q_blk = [_rope_apply(qk[:, p * LANES:(p + 1) * LANES], cos, sin, first_half, B_QK_DIM // 2)
             for p in range(nblk)]
    k_blk = [_rope_apply(qk[:, (nblk + p) * LANES:(nblk + p + 1) * LANES], cos, sin, first_half,
                         B_QK_DIM // 2) * (B_QK_DIM ** -0.5) for p in range(nblk)]
    for h in range(B_HEADS):
        p, half = divmod(h, 2)
        own = (lane >= 64 * half) & (lane < 64 * (half + 1))
        q = jnp.where(own, q_blk[p], 0.0)
        k = k_blk[p]
        v = v_ref[:, h * B_V_DIM:(h + 1) * B_V_DIM]
        s = _dot_nt(q.astype(BF16), k.astype(BF16)) * decay_ref[h]
        intra = _dot(s.astype(BF16), v)
        st = state_ref[h]
        cross = _dot((q * xi_ref[h]).astype(BF16), st.astype(BF16))
        u = _dot_tn((k * zeta_ref[h]).astype(BF16), v)
        state_ref[h] = gch_ref[h] * st + u
        o = intra + cross
        mu = jnp.mean(o, axis=-1, keepdims=True)
        d = o - mu
        var = jnp.mean(d * d, axis=-1, keepdims=True)
        r = d * lax.rsqrt(var + NORM_EPS)
        gate = g_ref[:, h * B_V_DIM:(h + 1) * B_V_DIM].astype(F32)
        o_ref[:, h * B_V_DIM:(h + 1) * B_V_DIM] = (r * _silu(gate)).astype(o_ref.dtype)


def _retention(main, posf, qk_blk, v_blk, g_blk):
    b, t, _ = main.shape
    c = RET_CHUNK
    half = B_QK_DIM // 2
    inv_freq = ROPE_BASE ** (-jnp.arange(half, dtype=F32) / half)
    invf = jnp.tile(inv_freq, LANES // half).reshape(1, LANES)
    gammas = 1.0 - 2.0 ** (-5.0 - jnp.arange(B_HEADS, dtype=F32))
    log_g = jnp.log(gammas)
    idx = jnp.arange(c)
    diff = idx[:, None] - idx[None, :]
    decay = jnp.where(diff[None] >= 0,
                      jnp.exp(diff[None].astype(F32) * log_g[:, None, None]), 0.0)
    zeta = jnp.exp((c - 1 - idx).astype(F32)[None, :] * log_g[:, None])
    xi = jnp.exp((idx + 1).astype(F32)[None, :] * log_g[:, None])
    zeta = jnp.broadcast_to(zeta[:, :, None], (B_HEADS, c, LANES))
    xi = jnp.broadcast_to(xi[:, :, None], (B_HEADS, c, LANES))
    gch = jnp.broadcast_to(jnp.exp(c * log_g)[:, None, None], (B_HEADS, LANES, LANES))
    w = B_HEADS * B_V_DIM
    const = lambda shape: pl.BlockSpec(shape, lambda bb, n: (0,) * len(shape))
    return pl.pallas_call(
        _retention_body,
        out_shape=jax.ShapeDtypeStruct((b, t, w), BF16),
        grid=(b, t // c),
        in_specs=[
            pl.BlockSpec((None, c, w), lambda bb, n: (bb, n, qk_blk)),
            pl.BlockSpec((None, c, w), lambda bb, n: (bb, n, v_blk)),
            pl.BlockSpec((None, c, w), lambda bb, n: (bb, n, g_blk)),
            pl.BlockSpec((None, c, 1), lambda bb, n: (bb, n, 0)),
            const((1, LANES)),
            const((B_HEADS, c, c)),
            const((B_HEADS, c, LANES)),
            const((B_HEADS, c, LANES)),
            const((B_HEADS, LANES, LANES)),
        ],
        out_specs=pl.BlockSpec((None, c, w), lambda bb, n: (bb, n, 0)),
        scratch_shapes=[pltpu.VMEM((B_HEADS, LANES, B_V_DIM), F32)],
        compiler_params=pltpu.CompilerParams(
            dimension_semantics=("arbitrary", "arbitrary"), vmem_limit_bytes=VMEM_LIMIT),
        name="retention",
    )(main, main, main, posf, invf, decay, zeta, xi, gch)


def _mla_prep_body(aux_ref, pos_ref, invf_ref, qn_ref, kvn_ref, wq_ref, wk_ref, wv_ref,
                   q_ref, k_ref, v_ref):
    rows = aux_ref.shape[0]
    aux = aux_ref[...]
    cq = aux[:, :D_Q_LORA]
    ckv = aux[:, D_Q_LORA:D_Q_LORA + D_KV_LORA]
    kr = aux[:, D_Q_LORA + D_KV_LORA:]
    cq = cq * lax.rsqrt(jnp.mean(cq * cq, axis=-1, keepdims=True) + NORM_EPS) * qn_ref[...]
    ckv = ckv * lax.rsqrt(jnp.mean(ckv * ckv, axis=-1, keepdims=True) + NORM_EPS) * kvn_ref[...]
    cos, sin = _rope_tables(pos_ref[...], invf_ref[...])
    lane = lax.broadcasted_iota(jnp.int32, (rows, LANES), 1)
    half = D_ROPE_DIM // 2
    first_half = lane < D_NOPE_DIM + half
    qf = _dot(cq.astype(BF16), wq_ref[...])
    kf = _dot(ckv.astype(BF16), wk_ref[...])
    kr = _rope_apply(kr, cos, sin, first_half, half)
    for h in range(D_HEADS):
        sl = slice(h * LANES, (h + 1) * LANES)
        q_ref[:, sl] = _rope_apply(qf[:, sl], cos, sin, first_half, half).astype(q_ref.dtype)
        k_ref[:, sl] = (kf[:, sl] + kr).astype(k_ref.dtype)
    v_ref[...] = _dot(ckv.astype(BF16), wv_ref[...]).astype(v_ref.dtype)


def _mla_prep(aux2, posf2, q_norm, kv_norm, w_uq, w_ukv):
    n = aux2.shape[0]
    half = D_ROPE_DIM // 2
    inv_freq = ROPE_BASE ** (-jnp.arange(half, dtype=F32) / half)
    invf = jnp.zeros((LANES,), F32).at[D_NOPE_DIM:D_NOPE_DIM + D_ROPE_DIM].set(jnp.tile(inv_freq, 2))
    dq = D_NOPE_DIM + D_ROPE_DIM
    wq = jnp.pad(w_uq.reshape(D_Q_LORA, D_HEADS, dq), ((0, 0), (0, 0), (0, LANES - dq)))
    wq = wq.reshape(D_Q_LORA, D_HEADS * LANES).astype(BF16)
    wkv = w_ukv.reshape(D_KV_LORA, D_HEADS, D_NOPE_DIM + D_V_DIM)
    wk = jnp.pad(wkv[:, :, :D_NOPE_DIM], ((0, 0), (0, 0), (0, LANES - D_NOPE_DIM)))
    wk = wk.reshape(D_KV_LORA, D_HEADS * LANES).astype(BF16)
    wv = wkv[:, :, D_NOPE_DIM:].reshape(D_KV_LORA, D_HEADS * D_V_DIM).astype(BF16)
    rows = PROJ_ROWS
    const = lambda shape: pl.BlockSpec(shape, lambda i: (0,) * len(shape))
    return pl.pallas_call(
        _mla_prep_body,
        out_shape=(jax.ShapeDtypeStruct((n, D_HEADS * LANES), BF16),
                   jax.ShapeDtypeStruct((n, D_HEADS * LANES), BF16),
                   jax.ShapeDtypeStruct((n, D_HEADS * D_V_DIM), BF16)),
        grid=(n // rows,),
        in_specs=[
            pl.BlockSpec((rows, aux2.shape[1]), lambda i: (i, 0)),
            pl.BlockSpec((rows, 1), lambda i: (i, 0)),
            const((1, LANES)),
            const((1, D_Q_LORA)),
            const((1, D_KV_LORA)),
            const(wq.shape), const(wk.shape), const(wv.shape),
        ],
        out_specs=(pl.BlockSpec((rows, D_HEADS * LANES), lambda i: (i, 0)),
                   pl.BlockSpec((rows, D_HEADS * LANES), lambda i: (i, 0)),
                   pl.BlockSpec((rows, D_HEADS * D_V_DIM), lambda i: (i, 0))),
        compiler_params=pltpu.CompilerParams(
            dimension_semantics=("arbitrary",), vmem_limit_bytes=VMEM_LIMIT),
        name="mla_prep",
    )(aux2, posf2, invf.reshape(1, LANES), q_norm.reshape(1, -1), kv_norm.reshape(1, -1),
      wq, wk, wv)


def _outproj_tail(m_lo, m_hi, w_ref, pg_ref, x_ref, o_ref):
    half = w_ref.shape[0] // 2
    y = _dot(m_lo, w_ref[:half, :]) + _dot(m_hi, w_ref[half:, :])
    y = y * lax.rsqrt(jnp.mean(y * y, axis=-1, keepdims=True) + NORM_EPS) * pg_ref[...]
    o_ref[...] = x_ref[...] + y


def _outproj_even_body(a_ref, ag_ref, r_ref, w_ref, pg_ref, x_ref, o_ref):
    a = a_ref[...] * _silu(ag_ref[...].astype(F32))
    _outproj_tail(a.astype(BF16), r_ref[...], w_ref, pg_ref, x_ref, o_ref)


def _outproj_odd_body(od_ref, cg_ref, om_ref, dg_ref, lam_ref, sub_ref, w_ref, pg_ref, x_ref,
                      o_ref, *, lam_init):
    lp = lam_ref[...]
    lam = (jnp.exp(jnp.sum(lp[0:1] * lp[1:2], axis=-1, keepdims=True))
           - jnp.exp(jnp.sum(lp[2:3] * lp[3:4], axis=-1, keepdims=True)) + lam_init)
    od = od_ref[...]
    cg = cg_ref[...].astype(F32)
    parts = []
    for h in range(C_HEADS):
        d = (od[:, h * LANES:(h + 1) * LANES]
             - lam * od[:, (C_HEADS + h) * LANES:(C_HEADS + h + 1) * LANES])
        d = d * lax.rsqrt(jnp.mean(d * d, axis=-1, keepdims=True) + NORM_EPS) * sub_ref[...]
        d = d * (1.0 - lam_init)
        parts.append((d * _silu(cg[:, h * LANES:(h + 1) * LANES])).astype(BF16))
    m_c = jnp.concatenate(parts, axis=1)
    m_d = (om_ref[...] * _silu(dg_ref[...].astype(F32))).astype(BF16)
    _outproj_tail(m_c, m_d, w_ref, pg_ref, x_ref, o_ref)


def _outproj_even(a2, main2, ag_blk, r2, w_out, post_gain, x2):
    n = x2.shape[0]
    rows = PROJ_ROWS
    w = a2.shape[1]
    return pl.pallas_call(
        _outproj_even_body,
        out_shape=jax.ShapeDtypeStruct((n, D_MODEL), F32),
        grid=(n // rows,),
        in_specs=[
            pl.BlockSpec((rows, w), lambda i: (i, 0)),
            pl.BlockSpec((rows, w), lambda i: (i, ag_blk)),
            pl.BlockSpec((rows, w), lambda i: (i, 0)),
            pl.BlockSpec((2 * w, D_MODEL), lambda i: (0, 0)),
            pl.BlockSpec((1, D_MODEL), lambda i: (0, 0)),
            pl.BlockSpec((rows, D_MODEL), lambda i: (i, 0)),
        ],
        out_specs=pl.BlockSpec((rows, D_MODEL), lambda i: (i, 0)),
        compiler_params=pltpu.CompilerParams(
            dimension_semantics=("arbitrary",), vmem_limit_bytes=VMEM_LIMIT),
        name="outproj_even",
    )(a2, main2, r2, w_out.astype(BF16), post_gain.reshape(1, D_MODEL), x2)


def _outproj_odd(od2, main2, cg_blk, om2, dg_blk, lam_params, subln, w_out, post_gain, x2,
                 lam_init):
    n = x2.shape[0]
    rows = PROJ_ROWS
    w = om2.shape[1]
    return pl.pallas_call(
        functools.partial(_outproj_odd_body, lam_init=lam_init),
        out_shape=jax.ShapeDtypeStruct((n, D_MODEL), F32),
        grid=(n // rows,),
        in_specs=[
            pl.BlockSpec((rows, od2.shape[1]), lambda i: (i, 0)),
            pl.BlockSpec((rows, w), lambda i: (i, cg_blk)),
            pl.BlockSpec((rows, w), lambda i: (i, 0)),
            pl.BlockSpec((rows, w), lambda i: (i, dg_blk)),
            pl.BlockSpec(lam_params.shape, lambda i: (0, 0)),
            pl.BlockSpec((1, C_V_DIM), lambda i: (0, 0)),
            pl.BlockSpec((2 * w, D_MODEL), lambda i: (0, 0)),
            pl.BlockSpec((1, D_MODEL), lambda i: (0, 0)),
            pl.BlockSpec((rows, D_MODEL), lambda i: (i, 0)),
        ],
        out_specs=pl.BlockSpec((rows, D_MODEL), lambda i: (i, 0)),
        compiler_params=pltpu.CompilerParams(
            dimension_semantics=("arbitrary",), vmem_limit_bytes=VMEM_LIMIT),
        name="outproj_odd",
    )(od2, main2, om2, main2, lam_params, subln.reshape(1, C_V_DIM), w_out.astype(BF16),
      post_gain.reshape(1, D_MODEL), x2)


def _cols(w, start, width):
    return w[:, start:start + width]


def _even_layer(x, posf, pre_gain, post_gain, w_in, w_out):
    b, t, _ = x.shape
    n = b * t
    aw = A_HEADS * A_HEAD_DIM
    iqw = IDX_HEADS * IDX_DIM
    o = 0
    aq, o = _cols(w_in, o, aw), o + aw
    ak, o = _cols(w_in, o, aw), o + aw
    av, o = _cols(w_in, o, aw), o + aw
    ag, o = _cols(w_in, o, aw), o + aw
    iq, o = _cols(w_in, o, iqw), o + iqw
    ik, o = _cols(w_in, o, IDX_DIM), o + IDX_DIM
    iw, o = _cols(w_in, o, IDX_HEADS), o + IDX_HEADS
    bq, o = _cols(w_in, o, B_HEADS * B_QK_DIM), o + B_HEADS * B_QK_DIM
    bk, o = _cols(w_in, o, B_HEADS * B_QK_DIM), o + B_HEADS * B_QK_DIM
    bv, o = _cols(w_in, o, B_HEADS * B_V_DIM), o + B_HEADS * B_V_DIM
    bg, o = _cols(w_in, o, B_HEADS * B_V_DIM), o + B_HEADS * B_V_DIM
    w_main = jnp.concatenate([aq, ak, av, ag, bq, bk, bv, bg], axis=1)
    zeros = lambda c: jnp.zeros((D_MODEL, c), F32)
    w_aux = jnp.concatenate([iq, ik, zeros(64), iw, zeros(LANES - IDX_HEADS)], axis=1)
    x2 = x.reshape(n, D_MODEL)
    main2, aux2 = _inproj(x2, pre_gain, w_main, w_aux)
    main = main2.reshape(b, t, -1)
    aux = aux2.reshape(b, t, -1)

    topk = min(DSA_TOPK, t // 4)
    bias = _dsa_select(aux, topk)
    heads = [(h // 2, h % 2, h // 2, h * A_HEAD_DIM, A_HEAD_DIM, h * A_HEAD_DIM)
             for h in range(A_HEADS)]
    av_t = jnp.swapaxes(main[:, :, 2 * aw:3 * aw], 1, 2)
    a = _flash(main, 0, aw, main, 1, aw, av_t, heads, A_HEAD_DIM ** -0.5, bias=bias)
    r = _retention(main, posf, 4, 5, 6)
    return _outproj_even(a.reshape(n, aw), main2, 3, r.reshape(n, -1), w_out, post_gain,
                         x2).reshape(b, t, D_MODEL)


def _odd_layer(x, posf, pre_gain, post_gain, w_in, w_out, lam_params, subln, q_norm, kv_norm,
               w_uq, w_ukv, layer):
    b, t, _ = x.shape
    n = b * t
    cw = C_HEADS * C_V_DIM
    o = 0
    cq, o = _cols(w_in, o, cw), o + cw
    ck, o = _cols(w_in, o, cw), o + cw
    cv, o = _cols(w_in, o, cw), o + cw
    cg, o = _cols(w_in, o, cw), o + cw
    dcq, o = _cols(w_in, o, D_Q_LORA), o + D_Q_LORA
    dckv, o = _cols(w_in, o, D_KV_LORA), o + D_KV_LORA
    dkr, o = _cols(w_in, o, D_ROPE_DIM), o + D_ROPE_DIM
    dg, o = _cols(w_in, o, cw), o + cw
    w_main = jnp.concatenate([cq, ck, cv, cg, dg], axis=1)
    zeros = lambda c: jnp.zeros((D_MODEL, c), F32)
    w_aux = jnp.concatenate([dcq, dckv, zeros(D_NOPE_DIM), dkr,
                             zeros(LANES - D_NOPE_DIM - D_ROPE_DIM)], axis=1)
    x2 = x.reshape(n, D_MODEL)
    main2, aux2 = _inproj(x2, pre_gain, w_main, w_aux)
    main = main2.reshape(b, t, -1)

    heads_c = [(h, m, h, h * C_V_DIM, C_V_DIM, (m * C_HEADS + h) * C_V_DIM)
               for m in range(2) for h in range(C_HEADS)]
    cv_t = jnp.swapaxes(main[:, :, 2 * cw:3 * cw], 1, 2)
    od = _flash(main, 0, cw, main, 1, cw, cv_t, heads_c, C_QK_DIM ** -0.5)

    qd, kd, vd = _mla_prep(aux2, posf.reshape(n, 1), q_norm, kv_norm, w_uq, w_ukv)
    heads_d = [(h, None, h, h * D_V_DIM, D_V_DIM, h * D_V_DIM) for h in range(D_HEADS)]
    vd_t = jnp.swapaxes(vd.reshape(b, t, -1), 1, 2)
    om = _flash(qd.reshape(b, t, -1), 0, D_HEADS * LANES, kd.reshape(b, t, -1), 0,
                D_HEADS * LANES, vd_t, heads_d, (D_NOPE_DIM + D_ROPE_DIM) ** -0.5)

    lam_init = 0.8 - 0.6 * math.exp(-0.3 * layer)
    return _outproj_odd(od.reshape(n, -1), main2, 3, om.reshape(n, -1), 4, lam_params, subln,
                        w_out, post_gain, x2, lam_init).reshape(b, t, D_MODEL)


def kernel(x, positions, pre_norm, post_norm, w_in_even, w_out_even, w_in_odd, diff_lambda,
           diff_subln, mla_q_norm, mla_kv_norm, mla_w_uq, mla_w_ukv, w_out_odd):
    b, t, _ = x.shape
    posf = positions.astype(F32).reshape(b, t, 1)
    depth = pre_norm.shape[0]
    for layer in range(depth):
        j = layer // 2
        if layer % 2 == 0:
            x = _even_layer(x, posf, pre_norm[layer], post_norm[layer], w_in_even[j],
                            w_out_even[j])
        else:
            x = _odd_layer(x, posf, pre_norm[layer], post_norm[layer], w_in_odd[j],
                           w_out_odd[j], diff_lambda[j], diff_subln[j], mla_q_norm[j],
                           mla_kv_norm[j], mla_w_uq[j], mla_w_ukv[j], layer)
    return x
```

```python
import functools
import math

import jax
import jax.numpy as jnp
import numpy as np
from jax import lax
from jax.experimental import pallas as pl
from jax.experimental.pallas import tpu as pltpu

F32 = jnp.float32
BF16 = jnp.bfloat16

D_MODEL = 1024
NORM_EPS = 1e-6
ROPE_BASE = 10000.0

A_HEADS, A_HEAD_DIM = 8, 64
IDX_HEADS, IDX_DIM = 4, 64
DSA_TOPK = 256
B_HEADS, B_QK_DIM, B_V_DIM = 4, 64, 128
RET_CHUNK = 128
C_HEADS, C_QK_DIM, C_V_DIM = 4, 64, 128
D_HEADS, D_NOPE_DIM, D_ROPE_DIM, D_V_DIM = 8, 64, 32, 64
D_Q_LORA, D_KV_LORA = 256, 128

LANES = 128
NEG_BIG = -1e30
LOG2E = math.log2(math.e)
ONES_ROWS = 16
INT_MIN = -(2 ** 31)
VMEM_LIMIT = 56 * 1024 * 1024

PROJ_ROWS = 512
SEL_Q = 256
SEL_K = 512
FLASH_T = 512


def _silu(x):
    return x * (1.0 / (1.0 + jnp.exp(-x)))


def _dot(a, b):
    return jnp.dot(a, b, preferred_element_type=F32)


def _dot_nt(a, b):
    return lax.dot_general(a, b, (((1,), (1,)), ((), ())), preferred_element_type=F32)


def _dot_tn(a, b):
    return lax.dot_general(a, b, (((0,), (0,)), ((), ())), preferred_element_type=F32)


def _split_bf16(x):
    hi = x.astype(BF16)
    lo = (x - hi.astype(F32)).astype(BF16)
    return hi, lo


def _inproj_body(x_ref, g_ref, w_ref, cs_ref, whi_ref, wlo_ref, o_ref, oaux_ref):
    x = x_ref[...]
    h = x * lax.rsqrt(jnp.mean(x * x, axis=-1, keepdims=True) + NORM_EPS) * g_ref[...]
    hb, hl = _split_bf16(h)
    o_ref[...] = (_dot(hb, w_ref[...]) * cs_ref[...]).astype(o_ref.dtype)
    whi = whi_ref[...]
    oaux_ref[...] = _dot(hb, whi) + _dot(hl, whi) + _dot(hb, wlo_ref[...])


def _inproj(x2, gain, w_main, col_scale, w_aux):
    n = x2.shape[0]
    cm, ca = w_main.shape[1], w_aux.shape[1]
    whi, wlo = _split_bf16(w_aux)
    return pl.pallas_call(
        _inproj_body,
        out_shape=(jax.ShapeDtypeStruct((n, cm), BF16), jax.ShapeDtypeStruct((n, ca), F32)),
        grid=(n // PROJ_ROWS,),
        in_specs=[
            pl.BlockSpec((PROJ_ROWS, D_MODEL), lambda i: (i, 0)),
            pl.BlockSpec((1, D_MODEL), lambda i: (0, 0)),
            pl.BlockSpec((D_MODEL, cm), lambda i: (0, 0)),
            pl.BlockSpec((1, cm), lambda i: (0, 0)),
            pl.BlockSpec((D_MODEL, ca), lambda i: (0, 0)),
            pl.BlockSpec((D_MODEL, ca), lambda i: (0, 0)),
        ],
        out_specs=(pl.BlockSpec((PROJ_ROWS, cm), lambda i: (i, 0)),
                   pl.BlockSpec((PROJ_ROWS, ca), lambda i: (i, 0))),
        compiler_params=pltpu.CompilerParams(
            dimension_semantics=("arbitrary",), vmem_limit_bytes=VMEM_LIMIT),
        name="inproj",
    )(x2, gain.reshape(1, D_MODEL), w_main.astype(BF16), col_scale.reshape(1, cm), whi, wlo)


def _query_col_scale(width, q_cols, scale):
    return jnp.ones((width,), F32).at[:q_cols].set(scale * LOG2E)


def _sortable_key(score):
    score = jnp.where(score == 0.0, 0.0, score)
    bits = pltpu.bitcast(score, jnp.int32)
    return bits ^ ((bits >> 31) & jnp.int32(0x7FFFFFFF))


FOLD_ROWS = 32


def _row_fold(x):
    acc = x[:FOLD_ROWS]
    for c in range(1, x.shape[0] // FOLD_ROWS):
        acc = acc + x[c * FOLD_ROWS:(c + 1) * FOLD_ROWS]
    return acc


def _dsa_select_body(aux_q_ref, aux_k_ref, tril_ref, bias_ref, keys_ref, *, topk, seq):
    i = pl.program_id(1)
    q0 = i * SEL_Q
    nkt = (q0 + SEL_Q + SEL_K - 1) // SEL_K
    n_all = seq // SEL_K

    aq = aux_q_ref[...]
    w_t = (aq[:, 384:512] * (IDX_HEADS ** -0.5 * IDX_DIM ** -0.5)).T
    w_rows = [w_t[h:h + 1, :] for h in range(IDX_HEADS)]
    q_ops = []
    for h in range(IDX_HEADS):
        qh = aq[:, h * IDX_DIM:(h + 1) * IDX_DIM]
        hi, lo = _split_bf16(qh)
        q_ops.append(jnp.concatenate([hi, lo, hi], axis=1))
    qpos = q0 + lax.broadcasted_iota(jnp.int32, (SEL_K, SEL_Q), 1)
    kiota = lax.broadcasted_iota(jnp.int32, (SEL_K, SEL_Q), 0)

    def key_rows(kt):
        return pl.ds(pl.multiple_of(kt * SEL_K, SEL_K), SEL_K)

    def score_tile(kt, carry):
        kk = aux_k_ref[key_rows(kt), 0:IDX_DIM]
        khi, klo = _split_bf16(kk)
        k_op = jnp.concatenate([khi, khi, klo], axis=1)
        score = jnp.zeros((SEL_K, SEL_Q), F32)
        for h in range(IDX_HEADS):
            rel = jnp.maximum(_dot_nt(k_op, q_ops[h]), 0.0)
            score = score + w_rows[h] * rel
        key = _sortable_key(score)
        key = jnp.where(kt * SEL_K + kiota <= qpos, key, INT_MIN)
        keys_ref[key_rows(kt), :] = key
        return carry

    lax.fori_loop(0, nkt, score_tile, 0)

    def count(pred_fn):
        def body(kt, acc):
            tile = keys_ref[key_rows(kt), :]
            return acc + _row_fold(jnp.where(pred_fn(tile), 1.0, 0.0))
        acc = lax.fori_loop(0, nkt, body, jnp.zeros((FOLD_ROWS, SEL_Q), F32))
        return jnp.sum(acc, axis=0, keepdims=True)

    def bit_pass(b, thr):
        cand = thr + (jnp.int32(1) << (31 - b))
        cnt = count(lambda t: t >= cand)
        return jnp.where(cnt >= float(topk), cand, thr)

    thr = lax.fori_loop(0, 32, bit_pass, jnp.full((1, SEL_Q), INT_MIN, jnp.int32))
    n_gt = count(lambda t: t > thr)
    need = jnp.where(thr == INT_MIN, 0.0, float(topk) - n_gt)
    tril = tril_ref[...]

    def emit_tile(kt, seen):
        tile = keys_ref[key_rows(kt), :]
        tie = tile == thr
        tie_f = jnp.where(tie, 1.0, 0.0)
        rank = seen + _dot(tril, tie_f.astype(BF16))
        sel = (tile > thr) | (tie & (rank <= need))
        bias_ref[key_rows(kt), :] = jnp.where(sel, 0.0, NEG_BIG).astype(bias_ref.dtype)
        return seen + jnp.sum(tie_f, axis=0, keepdims=True)

    lax.fori_loop(0, nkt, emit_tile, jnp.zeros((1, SEL_Q), F32))

    def fill_tile(kt, carry):
        bias_ref[key_rows(kt), :] = jnp.full((SEL_K, SEL_Q), NEG_BIG, bias_ref.dtype)
        return carry

    lax.fori_loop(nkt, n_all, fill_tile, 0)


def _dsa_select(aux, topk):
    b, t, ca = aux.shape
    tril = jnp.tril(jnp.ones((SEL_K, SEL_K), F32)).astype(BF16)
    return pl.pallas_call(
        functools.partial(_dsa_select_body, topk=topk, seq=t),
        out_shape=jax.ShapeDtypeStruct((b, t, t), BF16),
        grid=(b, t // SEL_Q),
        in_specs=[
            pl.BlockSpec((None, SEL_Q, ca), lambda bb, i: (bb, i, 0)),
            pl.BlockSpec((None, t, LANES), lambda bb, i: (bb, 0, 2)),
            pl.BlockSpec((SEL_K, SEL_K), lambda bb, i: (0, 0)),
        ],
        out_specs=pl.BlockSpec((None, t, SEL_Q), lambda bb, i: (bb, 0, i)),
        scratch_shapes=[pltpu.VMEM((t, SEL_Q), jnp.int32)],
        compiler_params=pltpu.CompilerParams(
            dimension_semantics=("arbitrary", "arbitrary"), vmem_limit_bytes=VMEM_LIMIT),
        name="dsa_select",
    )(aux, aux, tril)


def _flash_body(it_ref, jt_ref, *refs, heads, has_bias):
    if has_bias:
        q_ref, k_ref, vt_ref, bias_ref, o_ref, qm_ref, m_ref, l_ref, acc_ref = refs
    else:
        q_ref, k_ref, vt_ref, o_ref, qm_ref, m_ref, l_ref, acc_ref = refs
        bias_ref = None
    step = pl.program_id(1)
    i = it_ref[step]
    j = jt_ref[step]
    lane = lax.broadcasted_iota(jnp.int32, (FLASH_T, LANES), 1)

    @pl.when(j == 0)
    def _init():
        m_ref[...] = jnp.full(m_ref.shape, NEG_BIG, F32)
        l_ref[...] = jnp.zeros(l_ref.shape, F32)
        acc_ref[...] = jnp.zeros(acc_ref.shape, F32)
        for vh, (qb, half, _, _, _, _) in enumerate(heads):
            q = q_ref[:, qb * LANES:(qb + 1) * LANES]
            if half is not None:
                q = jnp.where((lane >= 64 * half) & (lane < 64 * (half + 1)), q, 0.0)
            qm_ref[vh] = q.astype(BF16)

    def tile(diag):
        if has_bias:
            bias = bias_ref[...].astype(F32)
        if diag:
            krow = lax.broadcasted_iota(jnp.int32, (FLASH_T, FLASH_T), 0)
            qcol = lax.broadcasted_iota(jnp.int32, (FLASH_T, FLASH_T), 1)
            causal = krow <= qcol

        def logits(vh):
            kb = heads[vh][2]
            s = _dot_nt(k_ref[:, kb * LANES:(kb + 1) * LANES], qm_ref[vh])
            if has_bias:
                s = s + bias
            if diag:
                s = jnp.where(causal, s, NEG_BIG)
            return s

        s_next = logits(0)
        for vh, (_, _, _, v0, nv, o0) in enumerate(heads):
            s = s_next
            if vh + 1 < len(heads):
                s_next = logits(vh + 1)
            m_prev = m_ref[vh]
            m_new = jnp.maximum(m_prev, jnp.max(s, axis=0, keepdims=True))
            alpha = jnp.exp2(m_prev - m_new)
            p = jnp.exp2(s - m_new).astype(BF16)
            pv = _dot(vt_ref[v0:v0 + nv + ONES_ROWS, :], p)
            acc_ref[o0:o0 + nv, :] = alpha * acc_ref[o0:o0 + nv, :] + pv[:nv]
            l_ref[vh] = alpha * l_ref[vh] + pv[nv:nv + 1]
            m_ref[vh] = m_new

    if has_bias:
        tile(False)
    else:
        @pl.when(j == i)
        def _diag():
            tile(True)

        @pl.when(j != i)
        def _off():
            tile(False)

    @pl.when(j == i)
    def _finish():
        for vh, (_, _, _, _, nv, o0) in enumerate(heads):
            acc_ref[o0:o0 + nv, :] = acc_ref[o0:o0 + nv, :] * (1.0 / l_ref[vh])
        o_ref[...] = acc_ref[...].T.astype(o_ref.dtype)


def _vt_with_ones(v, nv):
    b, t, c = v.shape
    vt = jnp.swapaxes(v, 1, 2).reshape(b, c // nv, nv, t)
    ones = jnp.ones((b, c // nv, ONES_ROWS, t), v.dtype)
    return jnp.concatenate([vt, ones], axis=2).reshape(b, -1, t)


def _flash(q_arr, q_blk, q_w, k_arr, k_blk, k_w, vt_arr, heads, bias=None):
    b, t, _ = q_arr.shape
    nt = t // FLASH_T
    pairs = [(i, j) for i in range(nt) for j in range(i + 1)]
    it = jnp.asarray(np.array([p[0] for p in pairs], np.int32))
    jt = jnp.asarray(np.array([p[1] for p in pairs], np.int32))
    nvh = len(heads)
    out_w = max(h[5] + h[4] for h in heads)
    vt_rows = vt_arr.shape[1]
    in_specs = [
        pl.BlockSpec((None, FLASH_T, q_w), lambda bb, s, it_r, jt_r: (bb, it_r[s], q_blk)),
        pl.BlockSpec((None, FLASH_T, k_w), lambda bb, s, it_r, jt_r: (bb, jt_r[s], k_blk)),
        pl.BlockSpec((None, vt_rows, FLASH_T), lambda bb, s, it_r, jt_r: (bb, 0, jt_r[s])),
    ]
    args = [q_arr, k_arr, vt_arr]
    if bias is not None:
        in_specs.append(pl.BlockSpec((None, FLASH_T, FLASH_T),
                                     lambda bb, s, it_r, jt_r: (bb, jt_r[s], it_r[s])))
        args.append(bias)
    return pl.pallas_call(
        functools.partial(_flash_body, heads=tuple(heads), has_bias=bias is not None),
        out_shape=jax.ShapeDtypeStruct((b, t, out_w), F32),
        grid_spec=pltpu.PrefetchScalarGridSpec(
            num_scalar_prefetch=2,
            grid=(b, len(pairs)),
            in_specs=in_specs,
            out_specs=pl.BlockSpec((None, FLASH_T, out_w),
                                   lambda bb, s, it_r, jt_r: (bb, it_r[s], 0)),
            scratch_shapes=[
                pltpu.VMEM((nvh, FLASH_T, LANES), BF16),
                pltpu.VMEM((nvh, 1, FLASH_T), F32),
                pltpu.VMEM((nvh, 1, FLASH_T), F32),
                pltpu.VMEM((out_w, FLASH_T), F32),
            ]),
        compiler_params=pltpu.CompilerParams(
            dimension_semantics=("arbitrary", "arbitrary"), vmem_limit_bytes=VMEM_LIMIT),
        name="flash_%d" % out_w + ("_bias" if bias is not None else ""),
    )(it, jt, *args)


def _rope_tables(pos_col, invf_row):
    ang = pos_col * invf_row
    return jnp.cos(ang), jnp.sin(ang)


def _rope_apply(x, cos, sin, first_half, half):
    fwd = pltpu.roll(x, LANES - half, 1)
    bwd = pltpu.roll(x, half, 1)
    return x * cos + jnp.where(first_half, -fwd, bwd) * sin


def _retention_body(qk_ref, v_ref, g_ref, pos_ref, invf_ref, decay_ref, zeta_ref, xi_ref,
                    gch_ref, o_ref, state_ref):
    c = RET_CHUNK

    @pl.when(pl.program_id(1) == 0)
    def _init():
        state_ref[...] = jnp.zeros(state_ref.shape, F32)

    lane = lax.broadcasted_iota(jnp.int32, (c, LANES), 1)
    first_half = (lane % B_QK_DIM) < (B_QK_DIM // 2)
    cos, sin = _rope_tables(pos_ref[...], invf_ref[...])
    qk = qk_ref[...].astype(F32)
    nblk = B_HEADS * B_QK_DIM // LANES
    q_blk = [_rope_apply(qk[:, p * LANES:(p + 1) * LANES], cos, sin, first_half, B_QK_DIM // 2)
             for p in range(nblk)]
    k_blk = [_rope_apply(qk[:, (nblk + p) * LANES:(nblk + p + 1) * LANES], cos, sin, first_half,
                         B_QK_DIM // 2) * (B_QK_DIM ** -0.5) for p in range(nblk)]
    for h in range(B_HEADS):
        p, half = divmod(h, 2)
        own = (lane >= 64 * half) & (lane < 64 * (half + 1))
        q = jnp.where(own, q_blk[p], 0.0)
        k = k_blk[p]
        v = v_ref[:, h * B_V_DIM:(h + 1) * B_V_DIM]
        s = _dot_nt(q.astype(BF16), k.astype(BF16)) * decay_ref[h]
        intra = _dot(s.astype(BF16), v)
        st = state_ref[h]
        cross = _dot((q * xi_ref[h]).astype(BF16), st.astype(BF16))
        u = _dot_tn((k * zeta_ref[h]).astype(BF16), v)
        state_ref[h] = gch_ref[h] * st + u
        o = intra + cross
        mu = jnp.mean(o, axis=-1, keepdims=True)
        d = o - mu
        var = jnp.mean(d * d, axis=-1, keepdims=True)
        r = d * lax.rsqrt(var + NORM_EPS)
        gate = g_ref[:, h * B_V_DIM:(h + 1) * B_V_DIM].astype(F32)
        o_ref[:, h * B_V_DIM:(h + 1) * B_V_DIM] = (r * _silu(gate)).astype(o_ref.dtype)


def _retention(main, posf, qk_blk, v_blk, g_blk):
    b, t, _ = main.shape
    c = RET_CHUNK
    half = B_QK_DIM // 2
    inv_freq = ROPE_BASE ** (-jnp.arange(half, dtype=F32) / half)
    invf = jnp.tile(inv_freq, LANES // half).reshape(1, LANES)
    gammas = 1.0 - 2.0 ** (-5.0 - jnp.arange(B_HEADS, dtype=F32))
    log_g = jnp.log(gammas)
    idx = jnp.arange(c)
    diff = idx[:, None] - idx[None, :]
    decay = jnp.where(diff[None] >= 0,
                      jnp.exp(diff[None].astype(F32) * log_g[:, None, None]), 0.0)
    zeta = jnp.exp((c - 1 - idx).astype(F32)[None, :] * log_g[:, None])
    xi = jnp.exp((idx + 1).astype(F32)[None, :] * log_g[:, None])
    zeta = jnp.broadcast_to(zeta[:, :, None], (B_HEADS, c, LANES))
    xi = jnp.broadcast_to(xi[:, :, None], (B_HEADS, c, LANES))
    gch = jnp.broadcast_to(jnp.exp(c * log_g)[:, None, None], (B_HEADS, LANES, LANES))
    w = B_HEADS * B_V_DIM
    const = lambda shape: pl.BlockSpec(shape, lambda bb, n: (0,) * len(shape))
    return pl.pallas_call(
        _retention_body,
        out_shape=jax.ShapeDtypeStruct((b, t, w), BF16),
        grid=(b, t // c),
        in_specs=[
            pl.BlockSpec((None, c, w), lambda bb, n: (bb, n, qk_blk)),
            pl.BlockSpec((None, c, w), lambda bb, n: (bb, n, v_blk)),
            pl.BlockSpec((None, c, w), lambda bb, n: (bb, n, g_blk)),
            pl.BlockSpec((None, c, 1), lambda bb, n: (bb, n, 0)),
            const((1, LANES)),
            const((B_HEADS, c, c)),
            const((B_HEADS, c, LANES)),
            const((B_HEADS, c, LANES)),
            const((B_HEADS, LANES, LANES)),
        ],
        out_specs=pl.BlockSpec((None, c, w), lambda bb, n: (bb, n, 0)),
        scratch_shapes=[pltpu.VMEM((B_HEADS, LANES, B_V_DIM), F32)],
        compiler_params=pltpu.CompilerParams(
            dimension_semantics=("arbitrary", "arbitrary"), vmem_limit_bytes=VMEM_LIMIT),
        name="retention",
    )(main, main, main, posf, invf, decay, zeta, xi, gch)


def _mla_prep_body(aux_ref, pos_ref, invf_ref, qn_ref, kvn_ref, wq_ref, wk_ref, wv_ref,
                   q_ref, k_ref, v_ref):
    rows = aux_ref.shape[0]
    aux = aux_ref[...]
    cq = aux[:, :D_Q_LORA]
    ckv = aux[:, D_Q_LORA:D_Q_LORA + D_KV_LORA]
    kr = aux[:, D_Q_LORA + D_KV_LORA:]
    cq = cq * lax.rsqrt(jnp.mean(cq * cq, axis=-1, keepdims=True) + NORM_EPS) * qn_ref[...]
    ckv = ckv * lax.rsqrt(jnp.mean(ckv * ckv, axis=-1, keepdims=True) + NORM_EPS) * kvn_ref[...]
    cos, sin = _rope_tables(pos_ref[...], invf_ref[...])
    lane = lax.broadcasted_iota(jnp.int32, (rows, LANES), 1)
    half = D_ROPE_DIM // 2
    first_half = lane < D_NOPE_DIM + half
    qf = _dot(cq.astype(BF16), wq_ref[...]) * ((D_NOPE_DIM + D_ROPE_DIM) ** -0.5 * LOG2E)
    kf = _dot(ckv.astype(BF16), wk_ref[...])
    kr = _rope_apply(kr, cos, sin, first_half, half)
    for h in range(D_HEADS):
        sl = slice(h * LANES, (h + 1) * LANES)
        q_ref[:, sl] = _rope_apply(qf[:, sl], cos, sin, first_half, half).astype(q_ref.dtype)
        k_ref[:, sl] = (kf[:, sl] + kr).astype(k_ref.dtype)
    v_ref[...] = _dot(ckv.astype(BF16), wv_ref[...]).astype(v_ref.dtype)


def _mla_prep(aux2, posf2, q_norm, kv_norm, w_uq, w_ukv):
    n = aux2.shape[0]
    half = D_ROPE_DIM // 2
    inv_freq = ROPE_BASE ** (-jnp.arange(half, dtype=F32) / half)
    invf = jnp.zeros((LANES,), F32).at[D_NOPE_DIM:D_NOPE_DIM + D_ROPE_DIM].set(jnp.tile(inv_freq, 2))
    dq = D_NOPE_DIM + D_ROPE_DIM
    wq = jnp.pad(w_uq.reshape(D_Q_LORA, D_HEADS, dq), ((0, 0), (0, 0), (0, LANES - dq)))
    wq = wq.reshape(D_Q_LORA, D_HEADS * LANES).astype(BF16)
    wkv = w_ukv.reshape(D_KV_LORA, D_HEADS, D_NOPE_DIM + D_V_DIM)
    wk = jnp.pad(wkv[:, :, :D_NOPE_DIM], ((0, 0), (0, 0), (0, LANES - D_NOPE_DIM)))
    wk = wk.reshape(D_KV_LORA, D_HEADS * LANES).astype(BF16)
    wv = wkv[:, :, D_NOPE_DIM:].reshape(D_KV_LORA, D_HEADS * D_V_DIM).astype(BF16)
    rows = PROJ_ROWS
    const = lambda shape: pl.BlockSpec(shape, lambda i: (0,) * len(shape))
    return pl.pallas_call(
        _mla_prep_body,
        out_shape=(jax.ShapeDtypeStruct((n, D_HEADS * LANES), BF16),
                   jax.ShapeDtypeStruct((n, D_HEADS * LANES), BF16),
                   jax.ShapeDtypeStruct((n, D_HEADS * D_V_DIM), BF16)),
        grid=(n // rows,),
        in_specs=[
            pl.BlockSpec((rows, aux2.shape[1]), lambda i: (i, 0)),
            pl.BlockSpec((rows, 1), lambda i: (i, 0)),
            const((1, LANES)),
            const((1, D_Q_LORA)),
            const((1, D_KV_LORA)),
            const(wq.shape), const(wk.shape), const(wv.shape),
        ],
        out_specs=(pl.BlockSpec((rows, D_HEADS * LANES), lambda i: (i, 0)),
                   pl.BlockSpec((rows, D_HEADS * LANES), lambda i: (i, 0)),
                   pl.BlockSpec((rows, D_HEADS * D_V_DIM), lambda i: (i, 0))),
        compiler_params=pltpu.CompilerParams(
            dimension_semantics=("arbitrary",), vmem_limit_bytes=VMEM_LIMIT),
        name="mla_prep",
    )(aux2, posf2, invf.reshape(1, LANES), q_norm.reshape(1, -1), kv_norm.reshape(1, -1),
      wq, wk, wv)


def _outproj_tail(m_lo, m_hi, w_ref, pg_ref, x_ref, o_ref):
    half = w_ref.shape[0] // 2
    y = _dot(m_lo, w_ref[:half, :]) + _dot(m_hi, w_ref[half:, :])
    y = y * lax.rsqrt(jnp.mean(y * y, axis=-1, keepdims=True) + NORM_EPS) * pg_ref[...]
    o_ref[...] = x_ref[...] + y


def _outproj_even_body(a_ref, ag_ref, r_ref, w_ref, pg_ref, x_ref, o_ref):
    a = a_ref[...] * _silu(ag_ref[...].astype(F32))
    _outproj_tail(a.astype(BF16), r_ref[...], w_ref, pg_ref, x_ref, o_ref)


def _outproj_odd_body(od_ref, cg_ref, om_ref, dg_ref, lam_ref, sub_ref, w_ref, pg_ref, x_ref,
                      o_ref, *, lam_init):
    lp = lam_ref[...]
    lam = (jnp.exp(jnp.sum(lp[0:1] * lp[1:2], axis=-1, keepdims=True))
           - jnp.exp(jnp.sum(lp[2:3] * lp[3:4], axis=-1, keepdims=True)) + lam_init)
    od = od_ref[...]
    cg = cg_ref[...].astype(F32)
    parts = []
    for h in range(C_HEADS):
        d = (od[:, h * LANES:(h + 1) * LANES]
             - lam * od[:, (C_HEADS + h) * LANES:(C_HEADS + h + 1) * LANES])
        d = d * lax.rsqrt(jnp.mean(d * d, axis=-1, keepdims=True) + NORM_EPS) * sub_ref[...]
        d = d * (1.0 - lam_init)
        parts.append((d * _silu(cg[:, h * LANES:(h + 1) * LANES])).astype(BF16))
    m_c = jnp.concatenate(parts, axis=1)
    m_d = (om_ref[...] * _silu(dg_ref[...].astype(F32))).astype(BF16)
    _outproj_tail(m_c, m_d, w_ref, pg_ref, x_ref, o_ref)


def _outproj_even(a2, main2, ag_blk, r2, w_out, post_gain, x2):
    n = x2.shape[0]
    rows = PROJ_ROWS
    w = a2.shape[1]
    return pl.pallas_call(
        _outproj_even_body,
        out_shape=jax.ShapeDtypeStruct((n, D_MODEL), F32),
        grid=(n // rows,),
        in_specs=[
            pl.BlockSpec((rows, w), lambda i: (i, 0)),
            pl.BlockSpec((rows, w), lambda i: (i, ag_blk)),
            pl.BlockSpec((rows, w), lambda i: (i, 0)),
            pl.BlockSpec((2 * w, D_MODEL), lambda i: (0, 0)),
            pl.BlockSpec((1, D_MODEL), lambda i: (0, 0)),
            pl.BlockSpec((rows, D_MODEL), lambda i: (i, 0)),
        ],
        out_specs=pl.BlockSpec((rows, D_MODEL), lambda i: (i, 0)),
        compiler_params=pltpu.CompilerParams(
            dimension_semantics=("arbitrary",), vmem_limit_bytes=VMEM_LIMIT),
        name="outproj_even",
    )(a2, main2, r2, w_out.astype(BF16), post_gain.reshape(1, D_MODEL), x2)


def _outproj_odd(od2, main2, cg_blk, om2, dg_blk, lam_params, subln, w_out, post_gain, x2,
                 lam_init):
    n = x2.shape[0]
    rows = PROJ_ROWS
    w = om2.shape[1]
    return pl.pallas_call(
        functools.partial(_outproj_odd_body, lam_init=lam_init),
        out_shape=jax.ShapeDtypeStruct((n, D_MODEL), F32),
        grid=(n // rows,),
        in_specs=[
            pl.BlockSpec((rows, od2.shape[1]), lambda i: (i, 0)),
            pl.BlockSpec((rows, w), lambda i: (i, cg_blk)),
            pl.BlockSpec((rows, w), lambda i: (i, 0)),
            pl.BlockSpec((rows, w), lambda i: (i, dg_blk)),
            pl.BlockSpec(lam_params.shape, lambda i: (0, 0)),
            pl.BlockSpec((1, C_V_DIM), lambda i: (0, 0)),
            pl.BlockSpec((2 * w, D_MODEL), lambda i: (0, 0)),
            pl.BlockSpec((1, D_MODEL), lambda i: (0, 0)),
            pl.BlockSpec((rows, D_MODEL), lambda i: (i, 0)),
        ],
        out_specs=pl.BlockSpec((rows, D_MODEL), lambda i: (i, 0)),
        compiler_params=pltpu.CompilerParams(
            dimension_semantics=("arbitrary",), vmem_limit_bytes=VMEM_LIMIT),
        name="outproj_odd",
    )(od2, main2, om2, main2, lam_params, subln.reshape(1, C_V_DIM), w_out.astype(BF16),
      post_gain.reshape(1, D_MODEL), x2)


def _cols(w, start, width):
    return w[:, start:start + width]


def _even_layer(x, posf, pre_gain, post_gain, w_in, w_out):
    b, t, _ = x.shape
    n = b * t
    aw = A_HEADS * A_HEAD_DIM
    iqw = IDX_HEADS * IDX_DIM
    o = 0
    aq, o = _cols(w_in, o, aw), o + aw
    ak, o = _cols(w_in, o, aw), o + aw
    av, o = _cols(w_in, o, aw), o + aw
    ag, o = _cols(w_in, o, aw), o + aw
    iq, o = _cols(w_in, o, iqw), o + iqw
    ik, o = _cols(w_in, o, IDX_DIM), o + IDX_DIM
    iw, o = _cols(w_in, o, IDX_HEADS), o + IDX_HEADS
    bq, o = _cols(w_in, o, B_HEADS * B_QK_DIM), o + B_HEADS * B_QK_DIM
    bk, o = _cols(w_in, o, B_HEADS * B_QK_DIM), o + B_HEADS * B_QK_DIM
    bv, o = _cols(w_in, o, B_HEADS * B_V_DIM), o + B_HEADS * B_V_DIM
    bg, o = _cols(w_in, o, B_HEADS * B_V_DIM), o + B_HEADS * B_V_DIM
    w_main = jnp.concatenate([aq, ak, av, ag, bq, bk, bv, bg], axis=1)
    zeros = lambda c: jnp.zeros((D_MODEL, c), F32)
    w_aux = jnp.concatenate([iq, ik, zeros(64), iw, zeros(LANES - IDX_HEADS)], axis=1)
    x2 = x.reshape(n, D_MODEL)
    col_scale = _query_col_scale(w_main.shape[1], aw, A_HEAD_DIM ** -0.5)
    main2, aux2 = _inproj(x2, pre_gain, w_main, col_scale, w_aux)
    main = main2.reshape(b, t, -1)
    aux = aux2.reshape(b, t, -1)

    topk = min(DSA_TOPK, t // 4)
    bias = _dsa_select(aux, topk)
    heads = [(h // 2, h % 2, h // 2, h * (A_HEAD_DIM + ONES_ROWS), A_HEAD_DIM, h * A_HEAD_DIM)
             for h in range(A_HEADS)]
    av_t = _vt_with_ones(main[:, :, 2 * aw:3 * aw], A_HEAD_DIM)
    a = _flash(main, 0, aw, main, 1, aw, av_t, heads, bias=bias)
    r = _retention(main, posf, 4, 5, 6)
    return _outproj_even(a.reshape(n, aw), main2, 3, r.reshape(n, -1), w_out, post_gain,
                         x2).reshape(b, t, D_MODEL)


def _odd_layer(x, posf, pre_gain, post_gain, w_in, w_out, lam_params, subln, q_norm, kv_norm,
               w_uq, w_ukv, layer):
    b, t, _ = x.shape
    n = b * t
    cw = C_HEADS * C_V_DIM
    o = 0
    cq, o = _cols(w_in, o, cw), o + cw
    ck, o = _cols(w_in, o, cw), o + cw
    cv, o = _cols(w_in, o, cw), o + cw
    cg, o = _cols(w_in, o, cw), o + cw
    dcq, o = _cols(w_in, o, D_Q_LORA), o + D_Q_LORA
    dckv, o = _cols(w_in, o, D_KV_LORA), o + D_KV_LORA
    dkr, o = _cols(w_in, o, D_ROPE_DIM), o + D_ROPE_DIM
    dg, o = _cols(w_in, o, cw), o + cw
    w_main = jnp.concatenate([cq, ck, cv, cg, dg], axis=1)
    zeros = lambda c: jnp.zeros((D_MODEL, c), F32)
    w_aux = jnp.concatenate([dcq, dckv, zeros(D_NOPE_DIM), dkr,
                             zeros(LANES - D_NOPE_DIM - D_ROPE_DIM)], axis=1)
    x2 = x.reshape(n, D_MODEL)
    col_scale = _query_col_scale(w_main.shape[1], cw, C_QK_DIM ** -0.5)
    main2, aux2 = _inproj(x2, pre_gain, w_main, col_scale, w_aux)
    main = main2.reshape(b, t, -1)

    heads_c = [(h, m, h, h * (C_V_DIM + ONES_ROWS), C_V_DIM, (m * C_HEADS + h) * C_V_DIM)
               for m in range(2) for h in range(C_HEADS)]
    cv_t = _vt_with_ones(main[:, :, 2 * cw:3 * cw], C_V_DIM)
    od = _flash(main, 0, cw, main, 1, cw, cv_t, heads_c)

    qd, kd, vd = _mla_prep(aux2, posf.reshape(n, 1), q_norm, kv_norm, w_uq, w_ukv)
    heads_d = [(h, None, h, h * (D_V_DIM + ONES_ROWS), D_V_DIM, h * D_V_DIM)
               for h in range(D_HEADS)]
    vd_t = _vt_with_ones(vd.reshape(b, t, -1), D_V_DIM)
    om = _flash(qd.reshape(b, t, -1), 0, D_HEADS * LANES, kd.reshape(b, t, -1), 0,
                D_HEADS * LANES, vd_t, heads_d)

    lam_init = 0.8 - 0.6 * math.exp(-0.3 * layer)
    return _outproj_odd(od.reshape(n, -1), main2, 3, om.reshape(n, -1), 4, lam_params, subln,
                        w_out, post_gain, x2, lam_init).reshape(b, t, D_MODEL)


def kernel(x, positions, pre_norm, post_norm, w_in_even, w_out_even, w_in_odd, diff_lambda,
           diff_subln, mla_q_norm, mla_kv_norm, mla_w_uq, mla_w_ukv, w_out_odd):
    b, t, _ = x.shape
    posf = positions.astype(F32).reshape(b, t, 1)
    depth = pre_norm.shape[0]
    for layer in range(depth):
        j = layer // 2
        if layer % 2 == 0:
            x = _even_layer(x, posf, pre_norm[layer], post_norm[layer], w_in_even[j],
                            w_out_even[j])
        else:
            x = _odd_layer(x, posf, pre_norm[layer], post_norm[layer], w_in_odd[j],
                           w_out_odd[j], diff_lambda[j], diff_subln[j], mla_q_norm[j],
                           mla_kv_norm[j], mla_w_uq[j], mla_w_ukv[j], layer)
    return x
```

```python
import functools
import math

import jax
import jax.numpy as jnp
import numpy as np
from jax import lax
from jax.experimental import pallas as pl
from jax.experimental.pallas import tpu as pltpu

F32 = jnp.float32
BF16 = jnp.bfloat16

D_MODEL = 1024
NORM_EPS = 1e-6
ROPE_BASE = 10000.0

A_HEADS, A_HEAD_DIM = 8, 64
IDX_HEADS, IDX_DIM = 4, 64
DSA_TOPK = 256
B_HEADS, B_QK_DIM, B_V_DIM = 4, 64, 128
RET_CHUNK = 128
C_HEADS, C_QK_DIM, C_V_DIM = 4, 64, 128
D_HEADS, D_NOPE_DIM, D_ROPE_DIM, D_V_DIM = 8, 64, 32, 64
D_Q_LORA, D_KV_LORA = 256, 128

LANES = 128
NEG_BIG = -1e30
LOG2E = math.log2(math.e)
ONES_ROWS = 16
INT_MIN = -(2 ** 31)
VMEM_LIMIT = 56 * 1024 * 1024

PROJ_ROWS = 512
SEL_Q = 256
SEL_K = 512
FLASH_T = 512


def _silu(x):
    return x * (1.0 / (1.0 + jnp.exp(-x)))


def _dot(a, b):
    return jnp.dot(a, b, preferred_element_type=F32)


def _dot_nt(a, b):
    return lax.dot_general(a, b, (((1,), (1,)), ((), ())), preferred_element_type=F32)


def _dot_tn(a, b):
    return lax.dot_general(a, b, (((0,), (0,)), ((), ())), preferred_element_type=F32)


def _split_bf16(x):
    hi = x.astype(BF16)
    lo = (x - hi.astype(F32)).astype(BF16)
    return hi, lo


def _inproj_body(x_ref, g_ref, w_ref, cs_ref, whi_ref, wlo_ref, o_ref, oaux_ref, *, aux_split):
    x = x_ref[...]
    h = x * lax.rsqrt(jnp.mean(x * x, axis=-1, keepdims=True) + NORM_EPS) * g_ref[...]
    hb, hl = _split_bf16(h)
    o_ref[...] = (_dot(hb, w_ref[...]) * cs_ref[...]).astype(o_ref.dtype)
    whi = whi_ref[...]
    aux = _dot(hb, whi)
    if aux_split:
        aux = aux + _dot(hl, whi) + _dot(hb, wlo_ref[...])
    oaux_ref[...] = aux


def _inproj(x2, gain, w_main, col_scale, w_aux, aux_split):
    n = x2.shape[0]
    cm, ca = w_main.shape[1], w_aux.shape[1]
    whi, wlo = _split_bf16(w_aux)
    return pl.pallas_call(
        functools.partial(_inproj_body, aux_split=aux_split),
        out_shape=(jax.ShapeDtypeStruct((n, cm), BF16), jax.ShapeDtypeStruct((n, ca), F32)),
        grid=(n // PROJ_ROWS,),
        in_specs=[
            pl.BlockSpec((PROJ_ROWS, D_MODEL), lambda i: (i, 0)),
            pl.BlockSpec((1, D_MODEL), lambda i: (0, 0)),
            pl.BlockSpec((D_MODEL, cm), lambda i: (0, 0)),
            pl.BlockSpec((1, cm), lambda i: (0, 0)),
            pl.BlockSpec((D_MODEL, ca), lambda i: (0, 0)),
            pl.BlockSpec((D_MODEL, ca), lambda i: (0, 0)),
        ],
        out_specs=(pl.BlockSpec((PROJ_ROWS, cm), lambda i: (i, 0)),
                   pl.BlockSpec((PROJ_ROWS, ca), lambda i: (i, 0))),
        compiler_params=pltpu.CompilerParams(
            dimension_semantics=("arbitrary",), vmem_limit_bytes=VMEM_LIMIT),
        name="inproj",
    )(x2, gain.reshape(1, D_MODEL), w_main.astype(BF16), col_scale.reshape(1, cm), whi, wlo)


def _query_col_scale(width, q_cols, scale):
    return jnp.ones((width,), F32).at[:q_cols].set(scale * LOG2E)


def _sortable_key(score):
    bits = pltpu.bitcast(score, jnp.int32)
    return jnp.where(bits < 0, INT_MIN - bits, bits)


FOLD_ROWS = 64
I16_MIN = -(2 ** 15)


def _dsa_select_body(aux_q_ref, aux_k_ref, tril_ref, bias_ref, keys_ref, hi_ref, lo_ref, *,
                     topk, seq):
    i = pl.program_id(1)
    q0 = i * SEL_Q
    nkt = (q0 + SEL_Q + SEL_K - 1) // SEL_K
    n_all = seq // SEL_K

    aq = aux_q_ref[...]
    w_t = (aq[:, 384:512] * (IDX_HEADS ** -0.5 * IDX_DIM ** -0.5)).T
    w_rows = [w_t[h:h + 1, :] for h in range(IDX_HEADS)]
    q_ops = []
    for h in range(IDX_HEADS):
        qh = aq[:, h * IDX_DIM:(h + 1) * IDX_DIM]
        hi, lo = _split_bf16(qh)
        q_ops.append(jnp.concatenate([hi, lo, hi], axis=1))
    qpos = q0 + lax.broadcasted_iota(jnp.int32, (SEL_K, SEL_Q), 1)
    kiota = lax.broadcasted_iota(jnp.int32, (SEL_K, SEL_Q), 0)

    def key_rows(kt):
        return pl.ds(pl.multiple_of(kt * SEL_K, SEL_K), SEL_K)

    def score_tile(kt, carry, *, masked):
        kk = aux_k_ref[key_rows(kt), 0:IDX_DIM]
        khi, klo = _split_bf16(kk)
        k_op = jnp.concatenate([khi, khi, klo], axis=1)
        score = jnp.zeros((SEL_K, SEL_Q), F32)
        for h in range(IDX_HEADS):
            rel = jnp.maximum(_dot_nt(k_op, q_ops[h]), 0.0)
            score = score + w_rows[h] * rel
        key = _sortable_key(score)
        if masked:
            key = jnp.where(kt * SEL_K + kiota <= qpos, key, INT_MIN)
        keys_ref[key_rows(kt), :] = key
        hi_ref[key_rows(kt), :] = (key >> 16).astype(jnp.int16)
        lo_ref[key_rows(kt), :] = key.astype(jnp.int16) ^ jnp.int16(I16_MIN)
        return carry

    n_full = q0 // SEL_K
    lax.fori_loop(0, n_full, functools.partial(score_tile, masked=False), 0)
    lax.fori_loop(n_full, nkt, functools.partial(score_tile, masked=True), 0)

    one16 = jnp.ones((FOLD_ROWS, SEL_Q), jnp.int16)
    zero16 = jnp.zeros((FOLD_ROWS, SEL_Q), jnp.int16)

    def count16(ref, cand, strict=False):
        cblk = jnp.broadcast_to(cand, (FOLD_ROWS, SEL_Q)).astype(jnp.int16)

        def body(kt, acc):
            base = kt * SEL_K
            for c in range(SEL_K // FOLD_ROWS):
                blk = ref[pl.ds(pl.multiple_of(base + c * FOLD_ROWS, FOLD_ROWS), FOLD_ROWS), :]
                hit = (blk > cblk) if strict else (blk >= cblk)
                acc = acc + jnp.where(hit, one16, zero16)
            return acc

        acc = lax.fori_loop(0, nkt, body, zero16)
        return jnp.sum(acc.astype(jnp.int32).astype(F32), axis=0, keepdims=True)

    def radix16(ref, rank):
        def bit_pass(b, thr):
            cand = thr + (jnp.int32(1) << (15 - b))
            return jnp.where(count16(ref, cand) >= rank, cand, thr)
        return lax.fori_loop(0, 16, bit_pass, jnp.full((1, SEL_Q), I16_MIN, jnp.int32))

    thr_hi = radix16(hi_ref, jnp.full((1, SEL_Q), float(topk), F32))
    n_gt_hi = count16(hi_ref, thr_hi, strict=True)
    thr_hi_blk = jnp.broadcast_to(thr_hi, (SEL_K, SEL_Q)).astype(jnp.int16)

    def rekey_tile(kt, carry):
        tie_hi = hi_ref[key_rows(kt), :] == thr_hi_blk
        hi_ref[key_rows(kt), :] = jnp.where(tie_hi, lo_ref[key_rows(kt), :],
                                            jnp.int16(I16_MIN))
        return carry

    lax.fori_loop(0, nkt, rekey_tile, 0)
    thr_lo = radix16(hi_ref, float(topk) - n_gt_hi)
    n_gt = n_gt_hi + count16(hi_ref, thr_lo, strict=True)
    thr = (thr_hi << 16) + (thr_lo - I16_MIN)
    need = jnp.where(thr == INT_MIN, 0.0, float(topk) - n_gt)
    tril = tril_ref[...]

    def emit_tile(kt, seen):
        tile = keys_ref[key_rows(kt), :]
        tie = tile == thr
        tie_f = jnp.where(tie, 1.0, 0.0)
        rank = seen + _dot(tril, tie_f.astype(BF16))
        order = jnp.where(tie, rank, jnp.where(tile > thr, -1.0, float(2 * seq)))
        bias_ref[key_rows(kt), :] = jnp.where(order <= need, 0.0, NEG_BIG).astype(bias_ref.dtype)
        return seen + jnp.sum(tie_f, axis=0, keepdims=True)

    lax.fori_loop(0, nkt, emit_tile, jnp.zeros((1, SEL_Q), F32))

    def fill_tile(kt, carry):
        bias_ref[key_rows(kt), :] = jnp.full((SEL_K, SEL_Q), NEG_BIG, bias_ref.dtype)
        return carry

    lax.fori_loop(nkt, n_all, fill_tile, 0)


def _dsa_select(aux, topk):
    b, t, ca = aux.shape
    tril = jnp.tril(jnp.ones((SEL_K, SEL_K), F32)).astype(BF16)
    return pl.pallas_call(
        functools.partial(_dsa_select_body, topk=topk, seq=t),
        out_shape=jax.ShapeDtypeStruct((b, t, t), BF16),
        grid=(b, t // SEL_Q),
        in_specs=[
            pl.BlockSpec((None, SEL_Q, ca), lambda bb, i: (bb, i, 0)),
            pl.BlockSpec((None, t, LANES), lambda bb, i: (bb, 0, 2)),
            pl.BlockSpec((SEL_K, SEL_K), lambda bb, i: (0, 0)),
        ],
        out_specs=pl.BlockSpec((None, t, SEL_Q), lambda bb, i: (bb, 0, i)),
        scratch_shapes=[pltpu.VMEM((t, SEL_Q), jnp.int32), pltpu.VMEM((t, SEL_Q), jnp.int16),
                        pltpu.VMEM((t, SEL_Q), jnp.int16)],
        compiler_params=pltpu.CompilerParams(
            dimension_semantics=("arbitrary", "arbitrary"), vmem_limit_bytes=VMEM_LIMIT),
        name="dsa_select",
    )(aux, aux, tril)


def _flash_body(it_ref, jt_ref, *refs, heads, has_bias):
    if has_bias:
        q_ref, k_ref, vt_ref, bias_ref, o_ref, qm_ref, m_ref, l_ref, acc_ref = refs
    else:
        q_ref, k_ref, vt_ref, o_ref, qm_ref, m_ref, l_ref, acc_ref = refs
        bias_ref = None
    step = pl.program_id(1)
    i = it_ref[step]
    j = jt_ref[step]
    lane = lax.broadcasted_iota(jnp.int32, (FLASH_T, LANES), 1)

    @pl.when(j == 0)
    def _init():
        m_ref[...] = jnp.full(m_ref.shape, NEG_BIG, F32)
        l_ref[...] = jnp.zeros(l_ref.shape, F32)
        acc_ref[...] = jnp.zeros(acc_ref.shape, F32)
        for vh, (qb, half, _, _, _, _) in enumerate(heads):
            q = q_ref[:, qb * LANES:(qb + 1) * LANES]
            if half is not None:
                q = jnp.where((lane >= 64 * half) & (lane < 64 * (half + 1)), q, 0.0)
            qm_ref[vh] = q.astype(BF16)

    def tiles(subtiles):
        if any(d for _, d in subtiles):
            krow = lax.broadcasted_iota(jnp.int32, (FLASH_T, FLASH_T), 0)
            qcol = lax.broadcasted_iota(jnp.int32, (FLASH_T, FLASH_T), 1)
            causal = krow <= qcol
        bias = {}
        if has_bias:
            for ks, _ in subtiles:
                bias[ks] = bias_ref[ks * FLASH_T:(ks + 1) * FLASH_T, :].astype(F32)
        items = [(ks, d, vh) for ks, d in subtiles for vh in range(len(heads))]

        def logits(item):
            ks, diag, vh = item
            kb = heads[vh][2]
            s = _dot_nt(k_ref[ks * FLASH_T:(ks + 1) * FLASH_T, kb * LANES:(kb + 1) * LANES],
                        qm_ref[vh])
            if has_bias:
                s = s + bias[ks]
            if diag:
                s = jnp.where(causal, s, NEG_BIG)
            return s

        s_next = logits(items[0])
        for n, (ks, _, vh) in enumerate(items):
            _, _, _, v0, nv, o0 = heads[vh]
            s = s_next
            if n + 1 < len(items):
                s_next = logits(items[n + 1])
            m_prev = m_ref[vh]
            m_new = jnp.maximum(m_prev, jnp.max(s, axis=0, keepdims=True))
            alpha = jnp.exp2(m_prev - m_new)
            p = jnp.exp2(s - m_new).astype(BF16)
            pv = _dot(vt_ref[v0:v0 + nv + ONES_ROWS, ks * FLASH_T:(ks + 1) * FLASH_T], p)
            acc_ref[o0:o0 + nv, :] = alpha * acc_ref[o0:o0 + nv, :] + pv[:nv]
            l_ref[vh] = alpha * l_ref[vh] + pv[nv:nv + 1]
            m_ref[vh] = m_new

    last = j == i // 2
    i_even = i % 2 == 0
    if has_bias:
        @pl.when(last & i_even)
        def _one():
            tiles([(0, False)])

        @pl.when(jnp.logical_not(last & i_even))
        def _two():
            tiles([(0, False), (1, False)])
    else:
        @pl.when(jnp.logical_not(last))
        def _below():
            tiles([(0, False), (1, False)])

        @pl.when(last & i_even)
        def _diag_first():
            tiles([(0, True)])

        @pl.when(last & jnp.logical_not(i_even))
        def _diag_second():
            tiles([(0, False), (1, True)])

    @pl.when(last)
    def _finish():
        for vh, (_, _, _, _, nv, o0) in enumerate(heads):
            acc_ref[o0:o0 + nv, :] = acc_ref[o0:o0 + nv, :] * (1.0 / l_ref[vh])
        o_ref[...] = acc_ref[...].T.astype(o_ref.dtype)


def _vt_with_ones(v, nv):
    b, t, c = v.shape
    vt = jnp.swapaxes(v, 1, 2).reshape(b, c // nv, nv, t)
    ones = jnp.ones((b, c // nv, ONES_ROWS, t), v.dtype)
    return jnp.concatenate([vt, ones], axis=2).reshape(b, -1, t)


def _flash(q_arr, q_blk, q_w, k_arr, k_blk, k_w, vt_arr, heads, bias=None):
    b, t, _ = q_arr.shape
    nt = t // FLASH_T
    pairs = [(i, j) for i in range(nt) for j in range(i // 2 + 1)]
    it = jnp.asarray(np.array([p[0] for p in pairs], np.int32))
    jt = jnp.asarray(np.array([p[1] for p in pairs], np.int32))
    nvh = len(heads)
    out_w = max(h[5] + h[4] for h in heads)
    vt_rows = vt_arr.shape[1]
    in_specs = [
        pl.BlockSpec((None, FLASH_T, q_w), lambda bb, s, it_r, jt_r: (bb, it_r[s], q_blk)),
        pl.BlockSpec((None, 2 * FLASH_T, k_w), lambda bb, s, it_r, jt_r: (bb, jt_r[s], k_blk)),
        pl.BlockSpec((None, vt_rows, 2 * FLASH_T), lambda bb, s, it_r, jt_r: (bb, 0, jt_r[s])),
    ]
    args = [q_arr, k_arr, vt_arr]
    if bias is not None:
        in_specs.append(pl.BlockSpec((None, 2 * FLASH_T, FLASH_T),
                                     lambda bb, s, it_r, jt_r: (bb, jt_r[s], it_r[s])))
        args.append(bias)
    return pl.pallas_call(
        functools.partial(_flash_body, heads=tuple(heads), has_bias=bias is not None),
        out_shape=jax.ShapeDtypeStruct((b, t, out_w), F32),
        grid_spec=pltpu.PrefetchScalarGridSpec(
            num_scalar_prefetch=2,
            grid=(b, len(pairs)),
            in_specs=in_specs,
            out_specs=pl.BlockSpec((None, FLASH_T, out_w),
                                   lambda bb, s, it_r, jt_r: (bb, it_r[s], 0)),
            scratch_shapes=[
                pltpu.VMEM((nvh, FLASH_T, LANES), BF16),
                pltpu.VMEM((nvh, 1, FLASH_T), F32),
                pltpu.VMEM((nvh, 1, FLASH_T), F32),
                pltpu.VMEM((out_w, FLASH_T), F32),
            ]),
        compiler_params=pltpu.CompilerParams(
            dimension_semantics=("arbitrary", "arbitrary"), vmem_limit_bytes=VMEM_LIMIT),
        name="flash_%d" % out_w + ("_bias" if bias is not None else ""),
    )(it, jt, *args)


def _rope_tables(pos_col, invf_row):
    ang = pos_col * invf_row
    return jnp.cos(ang), jnp.sin(ang)


def _rope_apply(x, cos, sin, first_half, half):
    fwd = pltpu.roll(x, LANES - half, 1)
    bwd = pltpu.roll(x, half, 1)
    return x * cos + jnp.where(first_half, -fwd, bwd) * sin


def _retention_body(qk_ref, v_ref, g_ref, pos_ref, invf_ref, decay_ref, zeta_ref, xi_ref,
                    gch_ref, o_ref, state_ref):
    c = RET_CHUNK

    @pl.when(pl.program_id(1) == 0)
    def _init():
        state_ref[...] = jnp.zeros(state_ref.shape, F32)

    lane = lax.broadcasted_iota(jnp.int32, (c, LANES), 1)
    first_half = (lane % B_QK_DIM) < (B_QK_DIM // 2)
    cos, sin = _rope_tables(pos_ref[...], invf_ref[...])
    qk = qk_ref[...].astype(F32)
    nblk = B_HEADS * B_QK_DIM // LANES
    q_blk = [_rope_apply(qk[:, p * LANES:(p + 1) * LANES], cos, sin, first_half, B_QK_DIM // 2)
             for p in range(nblk)]
    k_blk = [_rope_apply(qk[:, (nblk + p) * LANES:(nblk + p + 1) * LANES], cos, sin, first_half,
                         B_QK_DIM // 2) * (B_QK_DIM ** -0.5) for p in range(nblk)]
    for h in range(B_HEADS):
        p, half = divmod(h, 2)
        own = (lane >= 64 * half) & (lane < 64 * (half + 1))
        q = jnp.where(own, q_blk[p], 0.0)
        k = k_blk[p]
        v = v_ref[:, h * B_V_DIM:(h + 1) * B_V_DIM]
        s = _dot_nt(q.astype(BF16), k.astype(BF16)) * decay_ref[h]
        intra = _dot(s.astype(BF16), v)
        st = state_ref[h]
        cross = _dot((q * xi_ref[h]).astype(BF16), st.astype(BF16))
        u = _dot_tn((k * zeta_ref[h]).astype(BF16), v)
        state_ref[h] = gch_ref[h] * st + u
        o = intra + cross
        mu = jnp.mean(o, axis=-1, keepdims=True)
        d = o - mu
        var = jnp.mean(d * d, axis=-1, keepdims=True)
        r = d * lax.rsqrt(var + NORM_EPS)
        gate = g_ref[:, h * B_V_DIM:(h + 1) * B_V_DIM].astype(F32)
        o_ref[:, h * B_V_DIM:(h + 1) * B_V_DIM] = (r * _silu(gate)).astype(o_ref.dtype)


def _retention(main, posf, qk_blk, v_blk, g_blk):
    b, t, _ = main.shape
    c = RET_CHUNK
    half = B_QK_DIM // 2
    inv_freq = ROPE_BASE ** (-jnp.arange(half, dtype=F32) / half)
    invf = jnp.tile(inv_freq, LANES // half).reshape(1, LANES)
    gammas = 1.0 - 2.0 ** (-5.0 - jnp.arange(B_HEADS, dtype=F32))
    log_g = jnp.log(gammas)
    idx = jnp.arange(c)
    diff = idx[:, None] - idx[None, :]
    decay = jnp.where(diff[None] >= 0,
                      jnp.exp(diff[None].astype(F32) * log_g[:, None, None]), 0.0)
    zeta = jnp.exp((c - 1 - idx).astype(F32)[None, :] * log_g[:, None])
    xi = jnp.exp((idx + 1).astype(F32)[None, :] * log_g[:, None])
    zeta = jnp.broadcast_to(zeta[:, :, None], (B_HEADS, c, LANES))
    xi = jnp.broadcast_to(xi[:, :, None], (B_HEADS, c, LANES))
    gch = jnp.broadcast_to(jnp.exp(c * log_g)[:, None, None], (B_HEADS, LANES, LANES))
    w = B_HEADS * B_V_DIM
    const = lambda shape: pl.BlockSpec(shape, lambda bb, n: (0,) * len(shape))
    return pl.pallas_call(
        _retention_body,
        out_shape=jax.ShapeDtypeStruct((b, t, w), BF16),
        grid=(b, t // c),
        in_specs=[
            pl.BlockSpec((None, c, w), lambda bb, n: (bb, n, qk_blk)),
            pl.BlockSpec((None, c, w), lambda bb, n: (bb, n, v_blk)),
            pl.BlockSpec((None, c, w), lambda bb, n: (bb, n, g_blk)),
            pl.BlockSpec((None, c, 1), lambda bb, n: (bb, n, 0)),
            const((1, LANES)),
            const((B_HEADS, c, c)),
            const((B_HEADS, c, LANES)),
            const((B_HEADS, c, LANES)),
            const((B_HEADS, LANES, LANES)),
        ],
        out_specs=pl.BlockSpec((None, c, w), lambda bb, n: (bb, n, 0)),
        scratch_shapes=[pltpu.VMEM((B_HEADS, LANES, B_V_DIM), F32)],
        compiler_params=pltpu.CompilerParams(
            dimension_semantics=("arbitrary", "arbitrary"), vmem_limit_bytes=VMEM_LIMIT),
        name="retention",
    )(main, main, main, posf, invf, decay, zeta, xi, gch)


def _mla_prep_body(aux_ref, pos_ref, invf_ref, qn_ref, kvn_ref, wq_ref, wk_ref, wv_ref,
                   q_ref, k_ref, v_ref):
    rows = aux_ref.shape[0]
    aux = aux_ref[...]
    cq = aux[:, :D_Q_LORA]
    ckv = aux[:, D_Q_LORA:D_Q_LORA + D_KV_LORA]
    kr = aux[:, D_Q_LORA + D_KV_LORA:]
    cq = cq * lax.rsqrt(jnp.mean(cq * cq, axis=-1, keepdims=True) + NORM_EPS) * qn_ref[...]
    ckv = ckv * lax.rsqrt(jnp.mean(ckv * ckv, axis=-1, keepdims=True) + NORM_EPS) * kvn_ref[...]
    cos, sin = _rope_tables(pos_ref[...], invf_ref[...])
    lane = lax.broadcasted_iota(jnp.int32, (rows, LANES), 1)
    half = D_ROPE_DIM // 2
    first_half = lane < D_NOPE_DIM + half
    qf = _dot(cq.astype(BF16), wq_ref[...]) * ((D_NOPE_DIM + D_ROPE_DIM) ** -0.5 * LOG2E)
    kf = _dot(ckv.astype(BF16), wk_ref[...])
    kr = _rope_apply(kr, cos, sin, first_half, half)
    for h in range(D_HEADS):
        sl = slice(h * LANES, (h + 1) * LANES)
        q_ref[:, sl] = _rope_apply(qf[:, sl], cos, sin, first_half, half).astype(q_ref.dtype)
        k_ref[:, sl] = (kf[:, sl] + kr).astype(k_ref.dtype)
    v_ref[...] = _dot(ckv.astype(BF16), wv_ref[...]).astype(v_ref.dtype)


def _mla_prep(aux2, posf2, q_norm, kv_norm, w_uq, w_ukv):
    n = aux2.shape[0]
    half = D_ROPE_DIM // 2
    inv_freq = ROPE_BASE ** (-jnp.arange(half, dtype=F32) / half)
    invf = jnp.zeros((LANES,), F32).at[D_NOPE_DIM:D_NOPE_DIM + D_ROPE_DIM].set(jnp.tile(inv_freq, 2))
    dq = D_NOPE_DIM + D_ROPE_DIM
    wq = jnp.pad(w_uq.reshape(D_Q_LORA, D_HEADS, dq), ((0, 0), (0, 0), (0, LANES - dq)))
    wq = wq.reshape(D_Q_LORA, D_HEADS * LANES).astype(BF16)
    wkv = w_ukv.reshape(D_KV_LORA, D_HEADS, D_NOPE_DIM + D_V_DIM)
    wk = jnp.pad(wkv[:, :, :D_NOPE_DIM], ((0, 0), (0, 0), (0, LANES - D_NOPE_DIM)))
    wk = wk.reshape(D_KV_LORA, D_HEADS * LANES).astype(BF16)
    wv = wkv[:, :, D_NOPE_DIM:].reshape(D_KV_LORA, D_HEADS * D_V_DIM).astype(BF16)
    rows = PROJ_ROWS
    const = lambda shape: pl.BlockSpec(shape, lambda i: (0,) * len(shape))
    return pl.pallas_call(
        _mla_prep_body,
        out_shape=(jax.ShapeDtypeStruct((n, D_HEADS * LANES), BF16),
                   jax.ShapeDtypeStruct((n, D_HEADS * LANES), BF16),
                   jax.ShapeDtypeStruct((n, D_HEADS * D_V_DIM), BF16)),
        grid=(n // rows,),
        in_specs=[
            pl.BlockSpec((rows, aux2.shape[1]), lambda i: (i, 0)),
            pl.BlockSpec((rows, 1), lambda i: (i, 0)),
            const((1, LANES)),
            const((1, D_Q_LORA)),
            const((1, D_KV_LORA)),
            const(wq.shape), const(wk.shape), const(wv.shape),
        ],
        out_specs=(pl.BlockSpec((rows, D_HEADS * LANES), lambda i: (i, 0)),
                   pl.BlockSpec((rows, D_HEADS * LANES), lambda i: (i, 0)),
                   pl.BlockSpec((rows, D_HEADS * D_V_DIM), lambda i: (i, 0))),
        compiler_params=pltpu.CompilerParams(
            dimension_semantics=("arbitrary",), vmem_limit_bytes=VMEM_LIMIT),
        name="mla_prep",
    )(aux2, posf2, invf.reshape(1, LANES), q_norm.reshape(1, -1), kv_norm.reshape(1, -1),
      wq, wk, wv)


def _outproj_tail(m_lo, m_hi, w_ref, pg_ref, x_ref, o_ref):
    half = w_ref.shape[0] // 2
    y = _dot(m_lo, w_ref[:half, :]) + _dot(m_hi, w_ref[half:, :])
    y = y * lax.rsqrt(jnp.mean(y * y, axis=-1, keepdims=True) + NORM_EPS) * pg_ref[...]
    o_ref[...] = x_ref[...] + y


def _outproj_even_body(a_ref, ag_ref, r_ref, w_ref, pg_ref, x_ref, o_ref):
    a = a_ref[...] * _silu(ag_ref[...].astype(F32))
    _outproj_tail(a.astype(BF16), r_ref[...], w_ref, pg_ref, x_ref, o_ref)


def _outproj_odd_body(od_ref, cg_ref, om_ref, dg_ref, lam_ref, sub_ref, w_ref, pg_ref, x_ref,
                      o_ref, *, lam_init):
    lp = lam_ref[...]
    lam = (jnp.exp(jnp.sum(lp[0:1] * lp[1:2], axis=-1, keepdims=True))
           - jnp.exp(jnp.sum(lp[2:3] * lp[3:4], axis=-1, keepdims=True)) + lam_init)
    od = od_ref[...]
    cg = cg_ref[...].astype(F32)
    parts = []
    for h in range(C_HEADS):
        d = (od[:, h * LANES:(h + 1) * LANES]
             - lam * od[:, (C_HEADS + h) * LANES:(C_HEADS + h + 1) * LANES])
        d = d * lax.rsqrt(jnp.mean(d * d, axis=-1, keepdims=True) + NORM_EPS) * sub_ref[...]
        d = d * (1.0 - lam_init)
        parts.append((d * _silu(cg[:, h * LANES:(h + 1) * LANES])).astype(BF16))
    m_c = jnp.concatenate(parts, axis=1)
    m_d = (om_ref[...] * _silu(dg_ref[...].astype(F32))).astype(BF16)
    _outproj_tail(m_c, m_d, w_ref, pg_ref, x_ref, o_ref)


def _outproj_even(a2, main2, ag_blk, r2, w_out, post_gain, x2):
    n = x2.shape[0]
    rows = PROJ_ROWS
    w = a2.shape[1]
    return pl.pallas_call(
        _outproj_even_body,
        out_shape=jax.ShapeDtypeStruct((n, D_MODEL), F32),
        grid=(n // rows,),
        in_specs=[
            pl.BlockSpec((rows, w), lambda i: (i, 0)),
            pl.BlockSpec((rows, w), lambda i: (i, ag_blk)),
            pl.BlockSpec((rows, w), lambda i: (i, 0)),
            pl.BlockSpec((2 * w, D_MODEL), lambda i: (0, 0)),
            pl.BlockSpec((1, D_MODEL), lambda i: (0, 0)),
            pl.BlockSpec((rows, D_MODEL), lambda i: (i, 0)),
        ],
        out_specs=pl.BlockSpec((rows, D_MODEL), lambda i: (i, 0)),
        compiler_params=pltpu.CompilerParams(
            dimension_semantics=("arbitrary",), vmem_limit_bytes=VMEM_LIMIT),
        name="outproj_even",
    )(a2, main2, r2, w_out.astype(BF16), post_gain.reshape(1, D_MODEL), x2)


def _outproj_odd(od2, main2, cg_blk, om2, dg_blk, lam_params, subln, w_out, post_gain, x2,
                 lam_init):
    n = x2.shape[0]
    rows = PROJ_ROWS
    w = om2.shape[1]
    return pl.pallas_call(
        functools.partial(_outproj_odd_body, lam_init=lam_init),
        out_shape=jax.ShapeDtypeStruct((n, D_MODEL), F32),
        grid=(n // rows,),
        in_specs=[
            pl.BlockSpec((rows, od2.shape[1]), lambda i: (i, 0)),
            pl.BlockSpec((rows, w), lambda i: (i, cg_blk)),
            pl.BlockSpec((rows, w), lambda i: (i, 0)),
            pl.BlockSpec((rows, w), lambda i: (i, dg_blk)),
            pl.BlockSpec(lam_params.shape, lambda i: (0, 0)),
            pl.BlockSpec((1, C_V_DIM), lambda i: (0, 0)),
            pl.BlockSpec((2 * w, D_MODEL), lambda i: (0, 0)),
            pl.BlockSpec((1, D_MODEL), lambda i: (0, 0)),
            pl.BlockSpec((rows, D_MODEL), lambda i: (i, 0)),
        ],
        out_specs=pl.BlockSpec((rows, D_MODEL), lambda i: (i, 0)),
        compiler_params=pltpu.CompilerParams(
            dimension_semantics=("arbitrary",), vmem_limit_bytes=VMEM_LIMIT),
        name="outproj_odd",
    )(od2, main2, om2, main2, lam_params, subln.reshape(1, C_V_DIM), w_out.astype(BF16),
      post_gain.reshape(1, D_MODEL), x2)


def _cols(w, start, width):
    return w[:, start:start + width]


def _even_layer(x, posf, pre_gain, post_gain, w_in, w_out):
    b, t, _ = x.shape
    n = b * t
    aw = A_HEADS * A_HEAD_DIM
    iqw = IDX_HEADS * IDX_DIM
    o = 0
    aq, o = _cols(w_in, o, aw), o + aw
    ak, o = _cols(w_in, o, aw), o + aw
    av, o = _cols(w_in, o, aw), o + aw
    ag, o = _cols(w_in, o, aw), o + aw
    iq, o = _cols(w_in, o, iqw), o + iqw
    ik, o = _cols(w_in, o, IDX_DIM), o + IDX_DIM
    iw, o = _cols(w_in, o, IDX_HEADS), o + IDX_HEADS
    bq, o = _cols(w_in, o, B_HEADS * B_QK_DIM), o + B_HEADS * B_QK_DIM
    bk, o = _cols(w_in, o, B_HEADS * B_QK_DIM), o + B_HEADS * B_QK_DIM
    bv, o = _cols(w_in, o, B_HEADS * B_V_DIM), o + B_HEADS * B_V_DIM
    bg, o = _cols(w_in, o, B_HEADS * B_V_DIM), o + B_HEADS * B_V_DIM
    w_main = jnp.concatenate([aq, ak, av, ag, bq, bk, bv, bg], axis=1)
    zeros = lambda c: jnp.zeros((D_MODEL, c), F32)
    w_aux = jnp.concatenate([iq, ik, zeros(64), iw, zeros(LANES - IDX_HEADS)], axis=1)
    x2 = x.reshape(n, D_MODEL)
    col_scale = _query_col_scale(w_main.shape[1], aw, A_HEAD_DIM ** -0.5)
    main2, aux2 = _inproj(x2, pre_gain, w_main, col_scale, w_aux, aux_split=True)
    main = main2.reshape(b, t, -1)
    aux = aux2.reshape(b, t, -1)

    topk = min(DSA_TOPK, t // 4)
    bias = _dsa_select(aux, topk)
    heads = [(h // 2, h % 2, h // 2, h * (A_HEAD_DIM + ONES_ROWS), A_HEAD_DIM, h * A_HEAD_DIM)
             for h in range(A_HEADS)]
    av_t = _vt_with_ones(main[:, :, 2 * aw:3 * aw], A_HEAD_DIM)
    a = _flash(main, 0, aw, main, 1, aw, av_t, heads, bias=bias)
    r = _retention(main, posf, 4, 5, 6)
    return _outproj_even(a.reshape(n, aw), main2, 3, r.reshape(n, -1), w_out, post_gain,
                         x2).reshape(b, t, D_MODEL)


def _odd_layer(x, posf, pre_gain, post_gain, w_in, w_out, lam_params, subln, q_norm, kv_norm,
               w_uq, w_ukv, layer):
    b, t, _ = x.shape
    n = b * t
    cw = C_HEADS * C_V_DIM
    o = 0
    cq, o = _cols(w_in, o, cw), o + cw
    ck, o = _cols(w_in, o, cw), o + cw
    cv, o = _cols(w_in, o, cw), o + cw
    cg, o = _cols(w_in, o, cw), o + cw
    dcq, o = _cols(w_in, o, D_Q_LORA), o + D_Q_LORA
    dckv, o = _cols(w_in, o, D_KV_LORA), o + D_KV_LORA
    dkr, o = _cols(w_in, o, D_ROPE_DIM), o + D_ROPE_DIM
    dg, o = _cols(w_in, o, cw), o + cw
    w_main = jnp.concatenate([cq, ck, cv, cg, dg], axis=1)
    zeros = lambda c: jnp.zeros((D_MODEL, c), F32)
    w_aux = jnp.concatenate([dcq, dckv, zeros(D_NOPE_DIM), dkr,
                             zeros(LANES - D_NOPE_DIM - D_ROPE_DIM)], axis=1)
    x2 = x.reshape(n, D_MODEL)
    col_scale = _query_col_scale(w_main.shape[1], cw, C_QK_DIM ** -0.5)
    main2, aux2 = _inproj(x2, pre_gain, w_main, col_scale, w_aux, aux_split=False)
    main = main2.reshape(b, t, -1)

    heads_c = [(h, m, h, h * (C_V_DIM + ONES_ROWS), C_V_DIM, (m * C_HEADS + h) * C_V_DIM)
               for m in range(2) for h in range(C_HEADS)]
    cv_t = _vt_with_ones(main[:, :, 2 * cw:3 * cw], C_V_DIM)
    od = _flash(main, 0, cw, main, 1, cw, cv_t, heads_c)

    qd, kd, vd = _mla_prep(aux2, posf.reshape(n, 1), q_norm, kv_norm, w_uq, w_ukv)
    heads_d = [(h, None, h, h * (D_V_DIM + ONES_ROWS), D_V_DIM, h * D_V_DIM)
               for h in range(D_HEADS)]
    vd_t = _vt_with_ones(vd.reshape(b, t, -1), D_V_DIM)
    om = _flash(qd.reshape(b, t, -1), 0, D_HEADS * LANES, kd.reshape(b, t, -1), 0,
                D_HEADS * LANES, vd_t, heads_d)

    lam_init = 0.8 - 0.6 * math.exp(-0.3 * layer)
    return _outproj_odd(od.reshape(n, -1), main2, 3, om.reshape(n, -1), 4, lam_params, subln,
                        w_out, post_gain, x2, lam_init).reshape(b, t, D_MODEL)


def kernel(x, positions, pre_norm, post_norm, w_in_even, w_out_even, w_in_odd, diff_lambda,
           diff_subln, mla_q_norm, mla_kv_norm, mla_w_uq, mla_w_ukv, w_out_odd):
    b, t, _ = x.shape
    posf = positions.astype(F32).reshape(b, t, 1)
    depth = pre_norm.shape[0]
    for layer in range(depth):
        j = layer // 2
        if layer % 2 == 0:
            x = _even_layer(x, posf, pre_norm[layer], post_norm[layer], w_in_even[j],
                            w_out_even[j])
        else:
            x = _odd_layer(x, posf, pre_norm[layer], post_norm[layer], w_in_odd[j],
                           w_out_odd[j], diff_lambda[j], diff_subln[j], mla_q_norm[j],
                           mla_kv_norm[j], mla_w_uq[j], mla_w_ukv[j], layer)
    return x
```

```python
import functools
import math

import jax
import jax.numpy as jnp
import numpy as np
from jax import lax
from jax.experimental import pallas as pl
from jax.experimental.pallas import tpu as pltpu

F32 = jnp.float32
BF16 = jnp.bfloat16

D_MODEL = 1024
NORM_EPS = 1e-6
ROPE_BASE = 10000.0

A_HEADS, A_HEAD_DIM = 8, 64
IDX_HEADS, IDX_DIM = 4, 64
DSA_TOPK = 256
B_HEADS, B_QK_DIM, B_V_DIM = 4, 64, 128
RET_CHUNK = 128
C_HEADS, C_QK_DIM, C_V_DIM = 4, 64, 128
D_HEADS, D_NOPE_DIM, D_ROPE_DIM, D_V_DIM = 8, 64, 32, 64
D_Q_LORA, D_KV_LORA = 256, 128

LANES = 128
NEG_BIG = -1e30
LOG2E = math.log2(math.e)
ONES_ROWS = 16
INT_MIN = -(2 ** 31)
VMEM_LIMIT = 56 * 1024 * 1024

PROJ_ROWS = 512
SEL_Q = 256
SEL_K = 512
FLASH_T = 512


def _silu(x):
    return x * (1.0 / (1.0 + jnp.exp(-x)))


def _dot(a, b):
    return jnp.dot(a, b, preferred_element_type=F32)


def _dot_nt(a, b):
    return lax.dot_general(a, b, (((1,), (1,)), ((), ())), preferred_element_type=F32)


def _dot_tn(a, b):
    return lax.dot_general(a, b, (((0,), (0,)), ((), ())), preferred_element_type=F32)


def _split_bf16(x):
    hi = x.astype(BF16)
    lo = (x - hi.astype(F32)).astype(BF16)
    return hi, lo


def _inproj_body(x_ref, g_ref, w_ref, cs_ref, whi_ref, wlo_ref, wvt_ref, vone_ref,
                 o_ref, oaux_ref, ovt_ref, *, aux_split):
    x = x_ref[...]
    h = x * lax.rsqrt(jnp.mean(x * x, axis=-1, keepdims=True) + NORM_EPS) * g_ref[...]
    hb, hl = _split_bf16(h)
    o_ref[...] = (_dot(hb, w_ref[...]) * cs_ref[...]).astype(o_ref.dtype)
    whi = whi_ref[...]
    aux = _dot(hb, whi)
    if aux_split:
        aux = aux + _dot(hl, whi) + _dot(hb, wlo_ref[...])
    oaux_ref[...] = aux
    ovt_ref[...] = (_dot_nt(wvt_ref[...], hb) + vone_ref[...]).astype(ovt_ref.dtype)


def _vt_weights(w_v, nv):
    k, c = w_v.shape
    h = c // nv
    wt = jnp.swapaxes(w_v, 0, 1).reshape(h, nv, k)
    wt = jnp.concatenate([wt, jnp.zeros((h, ONES_ROWS, k), w_v.dtype)], axis=1)
    one = jnp.concatenate([jnp.zeros((h, nv, 1), F32), jnp.ones((h, ONES_ROWS, 1), F32)], axis=1)
    return wt.reshape(-1, k).astype(BF16), one.reshape(-1, 1)


def _inproj(x2, gain, w_main, col_scale, w_aux, aux_split, w_v, nv):
    n = x2.shape[0]
    cm, ca = w_main.shape[1], w_aux.shape[1]
    whi, wlo = _split_bf16(w_aux)
    wvt, vone = _vt_weights(w_v, nv)
    r = wvt.shape[0]
    return pl.pallas_call(
        functools.partial(_inproj_body, aux_split=aux_split),
        out_shape=(jax.ShapeDtypeStruct((n, cm), BF16), jax.ShapeDtypeStruct((n, ca), F32),
                   jax.ShapeDtypeStruct((r, n), BF16)),
        grid=(n // PROJ_ROWS,),
        in_specs=[
            pl.BlockSpec((PROJ_ROWS, D_MODEL), lambda i: (i, 0)),
            pl.BlockSpec((1, D_MODEL), lambda i: (0, 0)),
            pl.BlockSpec((D_MODEL, cm), lambda i: (0, 0)),
            pl.BlockSpec((1, cm), lambda i: (0, 0)),
            pl.BlockSpec((D_MODEL, ca), lambda i: (0, 0)),
            pl.BlockSpec((D_MODEL, ca), lambda i: (0, 0)),
            pl.BlockSpec((r, D_MODEL), lambda i: (0, 0)),
            pl.BlockSpec((r, 1), lambda i: (0, 0)),
        ],
        out_specs=(pl.BlockSpec((PROJ_ROWS, cm), lambda i: (i, 0)),
                   pl.BlockSpec((PROJ_ROWS, ca), lambda i: (i, 0)),
                   pl.BlockSpec((r, PROJ_ROWS), lambda i: (0, i))),
        compiler_params=pltpu.CompilerParams(
            dimension_semantics=("arbitrary",), vmem_limit_bytes=VMEM_LIMIT),
        name="inproj",
    )(x2, gain.reshape(1, D_MODEL), w_main.astype(BF16), col_scale.reshape(1, cm), whi, wlo,
      wvt, vone)


def _query_col_scale(width, q_cols, scale):
    return jnp.ones((width,), F32).at[:q_cols].set(scale * LOG2E)


def _sortable_key(score):
    bits = pltpu.bitcast(score, jnp.int32)
    return jnp.where(bits < 0, INT_MIN - bits, bits)


FOLD_ROWS = 64
I16_MIN = -(2 ** 15)


def _dsa_select_body(aux_q_ref, aux_k_ref, tril_ref, bias_ref, keys_ref, hi_ref, lo_ref, *,
                     topk, seq):
    i = pl.program_id(1)
    q0 = i * SEL_Q
    nkt = (q0 + SEL_Q + SEL_K - 1) // SEL_K
    n_all = seq // SEL_K

    aq = aux_q_ref[...]
    w_t = (aq[:, 384:512] * (IDX_HEADS ** -0.5 * IDX_DIM ** -0.5)).T
    w_rows = [w_t[h:h + 1, :] for h in range(IDX_HEADS)]
    q_ops = []
    for h in range(IDX_HEADS):
        qh = aq[:, h * IDX_DIM:(h + 1) * IDX_DIM]
        hi, lo = _split_bf16(qh)
        q_ops.append(jnp.concatenate([hi, lo, hi], axis=1))
    qpos = q0 + lax.broadcasted_iota(jnp.int32, (SEL_K, SEL_Q), 1)
    kiota = lax.broadcasted_iota(jnp.int32, (SEL_K, SEL_Q), 0)

    def key_rows(kt):
        return pl.ds(pl.multiple_of(kt * SEL_K, SEL_K), SEL_K)

    def score_tile(kt, carry, *, masked):
        kk = aux_k_ref[key_rows(kt), 0:IDX_DIM]
        khi, klo = _split_bf16(kk)
        k_op = jnp.concatenate([khi, khi, klo], axis=1)
        score = jnp.zeros((SEL_K, SEL_Q), F32)
        for h in range(IDX_HEADS):
            rel = jnp.maximum(_dot_nt(k_op, q_ops[h]), 0.0)
            score = score + w_rows[h] * rel
        key = _sortable_key(score)
        if masked:
            key = jnp.where(kt * SEL_K + kiota <= qpos, key, INT_MIN)
        keys_ref[key_rows(kt), :] = key
        hi_ref[key_rows(kt), :] = (key >> 16).astype(jnp.int16)
        lo_ref[key_rows(kt), :] = key.astype(jnp.int16) ^ jnp.int16(I16_MIN)
        return carry

    n_full = q0 // SEL_K
    lax.fori_loop(0, n_full, functools.partial(score_tile, masked=False), 0)
    lax.fori_loop(n_full, nkt, functools.partial(score_tile, masked=True), 0)

    one16 = jnp.ones((FOLD_ROWS, SEL_Q), jnp.int16)
    zero16 = jnp.zeros((FOLD_ROWS, SEL_Q), jnp.int16)

    def count16(ref, cand, strict=False):
        cblk = jnp.broadcast_to(cand, (FOLD_ROWS, SEL_Q)).astype(jnp.int16)

        def body(kt, acc):
            base = kt * SEL_K
            for c in range(SEL_K // FOLD_ROWS):
                blk = ref[pl.ds(pl.multiple_of(base + c * FOLD_ROWS, FOLD_ROWS), FOLD_ROWS), :]
                hit = (blk > cblk) if strict else (blk >= cblk)
                acc = acc + jnp.where(hit, one16, zero16)
            return acc

        acc = lax.fori_loop(0, nkt, body, zero16)
        return jnp.sum(acc.astype(jnp.int32).astype(F32), axis=0, keepdims=True)

    def radix16(ref, rank):
        def bit_pass(b, thr):
            cand = thr + (jnp.int32(1) << (15 - b))
            return jnp.where(count16(ref, cand) >= rank, cand, thr)
        return lax.fori_loop(0, 16, bit_pass, jnp.full((1, SEL_Q), I16_MIN, jnp.int32))

    thr_hi = radix16(hi_ref, jnp.full((1, SEL_Q), float(topk), F32))
    n_gt_hi = count16(hi_ref, thr_hi, strict=True)
    thr_hi_blk = jnp.broadcast_to(thr_hi, (SEL_K, SEL_Q)).astype(jnp.int16)

    def rekey_tile(kt, carry):
        tie_hi = hi_ref[key_rows(kt), :] == thr_hi_blk
        hi_ref[key_rows(kt), :] = jnp.where(tie_hi, lo_ref[key_rows(kt), :],
                                            jnp.int16(I16_MIN))
        return carry

    lax.fori_loop(0, nkt, rekey_tile, 0)
    thr_lo = radix16(hi_ref, float(topk) - n_gt_hi)
    n_gt = n_gt_hi + count16(hi_ref, thr_lo, strict=True)
    thr = (thr_hi << 16) + (thr_lo - I16_MIN)
    need = jnp.where(thr == INT_MIN, 0.0, float(topk) - n_gt)
    tril = tril_ref[...]

    def emit_tile(kt, seen):
        tile = keys_ref[key_rows(kt), :]
        tie = tile == thr
        tie_f = jnp.where(tie, 1.0, 0.0)
        rank = seen + _dot(tril, tie_f.astype(BF16))
        order = jnp.where(tie, rank, jnp.where(tile > thr, -1.0, float(2 * seq)))
        bias_ref[key_rows(kt), :] = jnp.where(order <= need, 0.0, NEG_BIG).astype(bias_ref.dtype)
        return seen + jnp.sum(tie_f, axis=0, keepdims=True)

    lax.fori_loop(0, nkt, emit_tile, jnp.zeros((1, SEL_Q), F32))

    def fill_tile(kt, carry):
        bias_ref[key_rows(kt), :] = jnp.full((SEL_K, SEL_Q), NEG_BIG, bias_ref.dtype)
        return carry

    lax.fori_loop(nkt, n_all, fill_tile, 0)


def _dsa_select(aux, topk):
    b, t, ca = aux.shape
    tril = jnp.tril(jnp.ones((SEL_K, SEL_K), F32)).astype(BF16)
    return pl.pallas_call(
        functools.partial(_dsa_select_body, topk=topk, seq=t),
        out_shape=jax.ShapeDtypeStruct((b, t, t), BF16),
        grid=(b, t // SEL_Q),
        in_specs=[
            pl.BlockSpec((None, SEL_Q, ca), lambda bb, i: (bb, i, 0)),
            pl.BlockSpec((None, t, LANES), lambda bb, i: (bb, 0, 2)),
            pl.BlockSpec((SEL_K, SEL_K), lambda bb, i: (0, 0)),
        ],
        out_specs=pl.BlockSpec((None, t, SEL_Q), lambda bb, i: (bb, 0, i)),
        scratch_shapes=[pltpu.VMEM((t, SEL_Q), jnp.int32), pltpu.VMEM((t, SEL_Q), jnp.int16),
                        pltpu.VMEM((t, SEL_Q), jnp.int16)],
        compiler_params=pltpu.CompilerParams(
            dimension_semantics=("arbitrary", "arbitrary"), vmem_limit_bytes=VMEM_LIMIT),
        name="dsa_select",
    )(aux, aux, tril)


def _flash_body(it_ref, jt_ref, *refs, heads, has_bias):
    if has_bias:
        q_ref, k_ref, vt_ref, bias_ref, o_ref, qm_ref, m_ref, l_ref, acc_ref = refs
    else:
        q_ref, k_ref, vt_ref, o_ref, qm_ref, m_ref, l_ref, acc_ref = refs
        bias_ref = None
    step = pl.program_id(1)
    i = it_ref[step]
    j = jt_ref[step]
    lane = lax.broadcasted_iota(jnp.int32, (FLASH_T, LANES), 1)

    @pl.when(j == 0)
    def _init():
        m_ref[...] = jnp.full(m_ref.shape, NEG_BIG, F32)
        l_ref[...] = jnp.zeros(l_ref.shape, F32)
        acc_ref[...] = jnp.zeros(acc_ref.shape, F32)
        for vh, (qb, half, _, _, _, _) in enumerate(heads):
            q = q_ref[:, qb * LANES:(qb + 1) * LANES]
            if half is not None:
                q = jnp.where((lane >= 64 * half) & (lane < 64 * (half + 1)), q, 0.0)
            qm_ref[vh] = q.astype(BF16)

    def key_span(ks):
        return pl.ds(pl.multiple_of((2 * j + ks) * FLASH_T, FLASH_T), FLASH_T)

    def tiles(subtiles):
        if any(d for _, d in subtiles):
            krow = lax.broadcasted_iota(jnp.int32, (FLASH_T, FLASH_T), 0)
            qcol = lax.broadcasted_iota(jnp.int32, (FLASH_T, FLASH_T), 1)
            causal = krow <= qcol
        bias = {}
        if has_bias:
            for ks, _ in subtiles:
                bias[ks] = bias_ref[ks * FLASH_T:(ks + 1) * FLASH_T, :].astype(F32)
        items = [(ks, d, vh) for ks, d in subtiles for vh in range(len(heads))]

        def logits(item):
            ks, diag, vh = item
            kb = heads[vh][2]
            s = _dot_nt(k_ref[key_span(ks), kb * LANES:(kb + 1) * LANES], qm_ref[vh])
            if has_bias:
                s = s + bias[ks]
            if diag:
                s = jnp.where(causal, s, NEG_BIG)
            return s

        s_next = logits(items[0])
        for n, (ks, _, vh) in enumerate(items):
            _, _, _, v0, nv, o0 = heads[vh]
            s = s_next
            if n + 1 < len(items):
                s_next = logits(items[n + 1])
            m_prev = m_ref[vh]
            m_new = jnp.maximum(m_prev, jnp.max(s, axis=0, keepdims=True))
            alpha = jnp.exp2(m_prev - m_new)
            p = jnp.exp2(s - m_new).astype(BF16)
            pv = _dot(vt_ref[v0:v0 + nv + ONES_ROWS, key_span(ks)], p)
            acc_ref[o0:o0 + nv, :] = alpha * acc_ref[o0:o0 + nv, :] + pv[:nv]
            l_ref[vh] = alpha * l_ref[vh] + pv[nv:nv + 1]
            m_ref[vh] = m_new

    last = j == i // 2
    i_even = i % 2 == 0
    if has_bias:
        @pl.when(last & i_even)
        def _one():
            tiles([(0, False)])

        @pl.when(jnp.logical_not(last & i_even))
        def _two():
            tiles([(0, False), (1, False)])
    else:
        @pl.when(jnp.logical_not(last))
        def _below():
            tiles([(0, False), (1, False)])

        @pl.when(last & i_even)
        def _diag_first():
            tiles([(0, True)])

        @pl.when(last & jnp.logical_not(i_even))
        def _diag_second():
            tiles([(0, False), (1, True)])

    @pl.when(last)
    def _finish():
        for vh, (_, _, _, _, nv, o0) in enumerate(heads):
            acc_ref[o0:o0 + nv, :] = acc_ref[o0:o0 + nv, :] * (1.0 / l_ref[vh])
        o_ref[...] = acc_ref[...].T.astype(o_ref.dtype)


def _flash(q_arr, q_blk, q_w, k_arr, k_blk, k_w, vt_arr, heads, bias=None):
    b, t, _ = q_arr.shape
    nt = t // FLASH_T
    pairs = [(i, j) for i in range(nt) for j in range(i // 2 + 1)]
    it = jnp.asarray(np.array([p[0] for p in pairs], np.int32))
    jt = jnp.asarray(np.array([p[1] for p in pairs], np.int32))
    nvh = len(heads)
    out_w = max(h[5] + h[4] for h in heads)
    vt_rows = vt_arr.shape[0]
    in_specs = [
        pl.BlockSpec((None, FLASH_T, q_w), lambda bb, s, it_r, jt_r: (bb, it_r[s], q_blk)),
        pl.BlockSpec((None, t, k_w), lambda bb, s, it_r, jt_r: (bb, 0, k_blk),
                     pipeline_mode=pl.Buffered(1)),
        pl.BlockSpec((vt_rows, t), lambda bb, s, it_r, jt_r: (0, bb),
                     pipeline_mode=pl.Buffered(1)),
    ]
    args = [q_arr, k_arr, vt_arr]
    if bias is not None:
        in_specs.append(pl.BlockSpec((None, 2 * FLASH_T, FLASH_T),
                                     lambda bb, s, it_r, jt_r: (bb, jt_r[s], it_r[s])))
        args.append(bias)
    return pl.pallas_call(
        functools.partial(_flash_body, heads=tuple(heads), has_bias=bias is not None),
        out_shape=jax.ShapeDtypeStruct((b, t, out_w), F32),
        grid_spec=pltpu.PrefetchScalarGridSpec(
            num_scalar_prefetch=2,
            grid=(b, len(pairs)),
            in_specs=in_specs,
            out_specs=pl.BlockSpec((None, FLASH_T, out_w),
                                   lambda bb, s, it_r, jt_r: (bb, it_r[s], 0)),
            scratch_shapes=[
                pltpu.VMEM((nvh, FLASH_T, LANES), BF16),
                pltpu.VMEM((nvh, 1, FLASH_T), F32),
                pltpu.VMEM((nvh, 1, FLASH_T), F32),
                pltpu.VMEM((out_w, FLASH_T), F32),
            ]),
        compiler_params=pltpu.CompilerParams(
            dimension_semantics=("arbitrary", "arbitrary"), vmem_limit_bytes=VMEM_LIMIT),
        name="flash_%d" % out_w + ("_bias" if bias is not None else ""),
    )(it, jt, *args)


def _rope_tables(pos_col, invf_row):
    ang = pos_col * invf_row
    return jnp.cos(ang), jnp.sin(ang)


def _rope_apply(x, cos, sin, first_half, half):
    fwd = pltpu.roll(x, LANES - half, 1)
    bwd = pltpu.roll(x, half, 1)
    return x * cos + jnp.where(first_half, -fwd, bwd) * sin


def _retention_body(qk_ref, v_ref, g_ref, pos_ref, invf_ref, decay_ref, zeta_ref, xi_ref,
                    gch_ref, o_ref, state_ref):
    c = RET_CHUNK

    @pl.when(pl.program_id(1) == 0)
    def _init():
        state_ref[...] = jnp.zeros(state_ref.shape, F32)

    lane = lax.broadcasted_iota(jnp.int32, (c, LANES), 1)
    first_half = (lane % B_QK_DIM) < (B_QK_DIM // 2)
    cos, sin = _rope_tables(pos_ref[...], invf_ref[...])
    qk = qk_ref[...].astype(F32)
    nblk = B_HEADS * B_QK_DIM // LANES
    q_blk = [_rope_apply(qk[:, p * LANES:(p + 1) * LANES], cos, sin, first_half, B_QK_DIM // 2)
             for p in range(nblk)]
    k_blk = [_rope_apply(qk[:, (nblk + p) * LANES:(nblk + p + 1) * LANES], cos, sin, first_half,
                         B_QK_DIM // 2) * (B_QK_DIM ** -0.5) for p in range(nblk)]
    for h in range(B_HEADS):
        p, half = divmod(h, 2)
        own = (lane >= 64 * half) & (lane < 64 * (half + 1))
        q = jnp.where(own, q_blk[p], 0.0)
        k = k_blk[p]
        v = v_ref[:, h * B_V_DIM:(h + 1) * B_V_DIM]
        s = _dot_nt(q.astype(BF16), k.astype(BF16)) * decay_ref[h]
        intra = _dot(s.astype(BF16), v)
        st = state_ref[h]
        cross = _dot((q * xi_ref[h]).astype(BF16), st.astype(BF16))
        u = _dot_tn((k * zeta_ref[h]).astype(BF16), v)
        state_ref[h] = gch_ref[h] * st + u
        o = intra + cross
        mu = jnp.mean(o, axis=-1, keepdims=True)
        d = o - mu
        var = jnp.mean(d * d, axis=-1, keepdims=True)
        r = d * lax.rsqrt(var + NORM_EPS)
        gate = g_ref[:, h * B_V_DIM:(h + 1) * B_V_DIM].astype(F32)
        o_ref[:, h * B_V_DIM:(h + 1) * B_V_DIM] = (r * _silu(gate)).astype(o_ref.dtype)


def _retention(main, posf, qk_blk, v_blk, g_blk):
    b, t, _ = main.shape
    c = RET_CHUNK
    half = B_QK_DIM // 2
    inv_freq = ROPE_BASE ** (-jnp.arange(half, dtype=F32) / half)
    invf = jnp.tile(inv_freq, LANES // half).reshape(1, LANES)
    gammas = 1.0 - 2.0 ** (-5.0 - jnp.arange(B_HEADS, dtype=F32))
    log_g = jnp.log(gammas)
    idx = jnp.arange(c)
    diff = idx[:, None] - idx[None, :]
    decay = jnp.where(diff[None] >= 0,
                      jnp.exp(diff[None].astype(F32) * log_g[:, None, None]), 0.0)
    zeta = jnp.exp((c - 1 - idx).astype(F32)[None, :] * log_g[:, None])
    xi = jnp.exp((idx + 1).astype(F32)[None, :] * log_g[:, None])
    zeta = jnp.broadcast_to(zeta[:, :, None], (B_HEADS, c, LANES))
    xi = jnp.broadcast_to(xi[:, :, None], (B_HEADS, c, LANES))
    gch = jnp.broadcast_to(jnp.exp(c * log_g)[:, None, None], (B_HEADS, LANES, LANES))
    w = B_HEADS * B_V_DIM
    const = lambda shape: pl.BlockSpec(shape, lambda bb, n: (0,) * len(shape))
    return pl.pallas_call(
        _retention_body,
        out_shape=jax.ShapeDtypeStruct((b, t, w), BF16),
        grid=(b, t // c),
        in_specs=[
            pl.BlockSpec((None, c, w), lambda bb, n: (bb, n, qk_blk)),
            pl.BlockSpec((None, c, w), lambda bb, n: (bb, n, v_blk)),
            pl.BlockSpec((None, c, w), lambda bb, n: (bb, n, g_blk)),
            pl.BlockSpec((None, c, 1), lambda bb, n: (bb, n, 0)),
            const((1, LANES)),
            const((B_HEADS, c, c)),
            const((B_HEADS, c, LANES)),
            const((B_HEADS, c, LANES)),
            const((B_HEADS, LANES, LANES)),
        ],
        out_specs=pl.BlockSpec((None, c, w), lambda bb, n: (bb, n, 0)),
        scratch_shapes=[pltpu.VMEM((B_HEADS, LANES, B_V_DIM), F32)],
        compiler_params=pltpu.CompilerParams(
            dimension_semantics=("arbitrary", "arbitrary"), vmem_limit_bytes=VMEM_LIMIT),
        name="retention",
    )(main, main, main, posf, invf, decay, zeta, xi, gch)


def _mla_prep_body(aux_ref, pos_ref, invf_ref, qn_ref, kvn_ref, wq_ref, wk_ref, wvt_ref,
                   vone_ref, q_ref, k_ref, vt_ref):
    rows = aux_ref.shape[0]
    aux = aux_ref[...]
    cq = aux[:, :D_Q_LORA]
    ckv = aux[:, D_Q_LORA:D_Q_LORA + D_KV_LORA]
    kr = aux[:, D_Q_LORA + D_KV_LORA:]
    cq = cq * lax.rsqrt(jnp.mean(cq * cq, axis=-1, keepdims=True) + NORM_EPS) * qn_ref[...]
    ckv = ckv * lax.rsqrt(jnp.mean(ckv * ckv, axis=-1, keepdims=True) + NORM_EPS) * kvn_ref[...]
    cos, sin = _rope_tables(pos_ref[...], invf_ref[...])
    lane = lax.broadcasted_iota(jnp.int32, (rows, LANES), 1)
    half = D_ROPE_DIM // 2
    first_half = lane < D_NOPE_DIM + half
    qf = _dot(cq.astype(BF16), wq_ref[...]) * ((D_NOPE_DIM + D_ROPE_DIM) ** -0.5 * LOG2E)
    kf = _dot(ckv.astype(BF16), wk_ref[...])
    kr = _rope_apply(kr, cos, sin, first_half, half)
    for h in range(D_HEADS):
        sl = slice(h * LANES, (h + 1) * LANES)
        q_ref[:, sl] = _rope_apply(qf[:, sl], cos, sin, first_half, half).astype(q_ref.dtype)
        k_ref[:, sl] = (kf[:, sl] + kr).astype(k_ref.dtype)
    vt_ref[...] = (_dot_nt(wvt_ref[...], ckv.astype(BF16)) + vone_ref[...]).astype(vt_ref.dtype)


def _mla_prep(aux2, posf2, q_norm, kv_norm, w_uq, w_ukv):
    n = aux2.shape[0]
    half = D_ROPE_DIM // 2
    inv_freq = ROPE_BASE ** (-jnp.arange(half, dtype=F32) / half)
    invf = jnp.zeros((LANES,), F32).at[D_NOPE_DIM:D_NOPE_DIM + D_ROPE_DIM].set(jnp.tile(inv_freq, 2))
    dq = D_NOPE_DIM + D_ROPE_DIM
    wq = jnp.pad(w_uq.reshape(D_Q_LORA, D_HEADS, dq), ((0, 0), (0, 0), (0, LANES - dq)))
    wq = wq.reshape(D_Q_LORA, D_HEADS * LANES).astype(BF16)
    wkv = w_ukv.reshape(D_KV_LORA, D_HEADS, D_NOPE_DIM + D_V_DIM)
    wk = jnp.pad(wkv[:, :, :D_NOPE_DIM], ((0, 0), (0, 0), (0, LANES - D_NOPE_DIM)))
    wk = wk.reshape(D_KV_LORA, D_HEADS * LANES).astype(BF16)
    wvt, vone = _vt_weights(wkv[:, :, D_NOPE_DIM:].reshape(D_KV_LORA, D_HEADS * D_V_DIM), D_V_DIM)
    vt_rows = wvt.shape[0]
    rows = PROJ_ROWS
    const = lambda shape: pl.BlockSpec(shape, lambda i: (0,) * len(shape))
    return pl.pallas_call(
        _mla_prep_body,
        out_shape=(jax.ShapeDtypeStruct((n, D_HEADS * LANES), BF16),
                   jax.ShapeDtypeStruct((n, D_HEADS * LANES), BF16),
                   jax.ShapeDtypeStruct((vt_rows, n), BF16)),
        grid=(n // rows,),
        in_specs=[
            pl.BlockSpec((rows, aux2.shape[1]), lambda i: (i, 0)),
            pl.BlockSpec((rows, 1), lambda i: (i, 0)),
            const((1, LANES)),
            const((1, D_Q_LORA)),
            const((1, D_KV_LORA)),
            const(wq.shape), const(wk.shape), const(wvt.shape), const(vone.shape),
        ],
        out_specs=(pl.BlockSpec((rows, D_HEADS * LANES), lambda i: (i, 0)),
                   pl.BlockSpec((rows, D_HEADS * LANES), lambda i: (i, 0)),
                   pl.BlockSpec((vt_rows, rows), lambda i: (0, i))),
        compiler_params=pltpu.CompilerParams(
            dimension_semantics=("arbitrary",), vmem_limit_bytes=VMEM_LIMIT),
        name="mla_prep",
    )(aux2, posf2, invf.reshape(1, LANES), q_norm.reshape(1, -1), kv_norm.reshape(1, -1),
      wq, wk, wvt, vone)


def _outproj_tail(m_lo, m_hi, w_ref, pg_ref, x_ref, o_ref):
    half = w_ref.shape[0] // 2
    y = _dot(m_lo, w_ref[:half, :]) + _dot(m_hi, w_ref[half:, :])
    y = y * lax.rsqrt(jnp.mean(y * y, axis=-1, keepdims=True) + NORM_EPS) * pg_ref[...]
    o_ref[...] = x_ref[...] + y


def _outproj_even_body(a_ref, ag_ref, r_ref, w_ref, pg_ref, x_ref, o_ref):
    a = a_ref[...] * _silu(ag_ref[...].astype(F32))
    _outproj_tail(a.astype(BF16), r_ref[...], w_ref, pg_ref, x_ref, o_ref)


def _outproj_odd_body(od_ref, cg_ref, om_ref, dg_ref, lam_ref, sub_ref, w_ref, pg_ref, x_ref,
                      o_ref, *, lam_init):
    lp = lam_ref[...]
    lam = (jnp.exp(jnp.sum(lp[0:1] * lp[1:2], axis=-1, keepdims=True))
           - jnp.exp(jnp.sum(lp[2:3] * lp[3:4], axis=-1, keepdims=True)) + lam_init)
    od = od_ref[...]
    cg = cg_ref[...].astype(F32)
    parts = []
    for h in range(C_HEADS):
        d = (od[:, h * LANES:(h + 1) * LANES]
             - lam * od[:, (C_HEADS + h) * LANES:(C_HEADS + h + 1) * LANES])
        d = d * lax.rsqrt(jnp.mean(d * d, axis=-1, keepdims=True) + NORM_EPS) * sub_ref[...]
        d = d * (1.0 - lam_init)
        parts.append((d * _silu(cg[:, h * LANES:(h + 1) * LANES])).astype(BF16))
    m_c = jnp.concatenate(parts, axis=1)
    m_d = (om_ref[...] * _silu(dg_ref[...].astype(F32))).astype(BF16)
    _outproj_tail(m_c, m_d, w_ref, pg_ref, x_ref, o_ref)


def _outproj_even(a2, main2, ag_blk, r2, w_out, post_gain, x2):
    n = x2.shape[0]
    rows = PROJ_ROWS
    w = a2.shape[1]
    return pl.pallas_call(
        _outproj_even_body,
        out_shape=jax.ShapeDtypeStruct((n, D_MODEL), F32),
        grid=(n // rows,),
        in_specs=[
            pl.BlockSpec((rows, w), lambda i: (i, 0)),
            pl.BlockSpec((rows, w), lambda i: (i, ag_blk)),
            pl.BlockSpec((rows, w), lambda i: (i, 0)),
            pl.BlockSpec((2 * w, D_MODEL), lambda i: (0, 0)),
            pl.BlockSpec((1, D_MODEL), lambda i: (0, 0)),
            pl.BlockSpec((rows, D_MODEL), lambda i: (i, 0)),
        ],
        out_specs=pl.BlockSpec((rows, D_MODEL), lambda i: (i, 0)),
        compiler_params=pltpu.CompilerParams(
            dimension_semantics=("arbitrary",), vmem_limit_bytes=VMEM_LIMIT),
        name="outproj_even",
    )(a2, main2, r2, w_out.astype(BF16), post_gain.reshape(1, D_MODEL), x2)


def _outproj_odd(od2, main2, cg_blk, om2, dg_blk, lam_params, subln, w_out, post_gain, x2,
                 lam_init):
    n = x2.shape[0]
    rows = PROJ_ROWS
    w = om2.shape[1]
    return pl.pallas_call(
        functools.partial(_outproj_odd_body, lam_init=lam_init),
        out_shape=jax.ShapeDtypeStruct((n, D_MODEL), F32),
        grid=(n // rows,),
        in_specs=[
            pl.BlockSpec((rows, od2.shape[1]), lambda i: (i, 0)),
            pl.BlockSpec((rows, w), lambda i: (i, cg_blk)),
            pl.BlockSpec((rows, w), lambda i: (i, 0)),
            pl.BlockSpec((rows, w), lambda i: (i, dg_blk)),
            pl.BlockSpec(lam_params.shape, lambda i: (0, 0)),
            pl.BlockSpec((1, C_V_DIM), lambda i: (0, 0)),
            pl.BlockSpec((2 * w, D_MODEL), lambda i: (0, 0)),
            pl.BlockSpec((1, D_MODEL), lambda i: (0, 0)),
            pl.BlockSpec((rows, D_MODEL), lambda i: (i, 0)),
        ],
        out_specs=pl.BlockSpec((rows, D_MODEL), lambda i: (i, 0)),
        compiler_params=pltpu.CompilerParams(
            dimension_semantics=("arbitrary",), vmem_limit_bytes=VMEM_LIMIT),
        name="outproj_odd",
    )(od2, main2, om2, main2, lam_params, subln.reshape(1, C_V_DIM), w_out.astype(BF16),
      post_gain.reshape(1, D_MODEL), x2)


def _cols(w, start, width):
    return w[:, start:start + width]


def _even_layer(x, posf, pre_gain, post_gain, w_in, w_out):
    b, t, _ = x.shape
    n = b * t
    aw = A_HEADS * A_HEAD_DIM
    iqw = IDX_HEADS * IDX_DIM
    o = 0
    aq, o = _cols(w_in, o, aw), o + aw
    ak, o = _cols(w_in, o, aw), o + aw
    av, o = _cols(w_in, o, aw), o + aw
    ag, o = _cols(w_in, o, aw), o + aw
    iq, o = _cols(w_in, o, iqw), o + iqw
    ik, o = _cols(w_in, o, IDX_DIM), o + IDX_DIM
    iw, o = _cols(w_in, o, IDX_HEADS), o + IDX_HEADS
    bq, o = _cols(w_in, o, B_HEADS * B_QK_DIM), o + B_HEADS * B_QK_DIM
    bk, o = _cols(w_in, o, B_HEADS * B_QK_DIM), o + B_HEADS * B_QK_DIM
    bv, o = _cols(w_in, o, B_HEADS * B_V_DIM), o + B_HEADS * B_V_DIM
    bg, o = _cols(w_in, o, B_HEADS * B_V_DIM), o + B_HEADS * B_V_DIM
    w_main = jnp.concatenate([aq, ak, ag, bq, bk, bv, bg], axis=1)
    zeros = lambda c: jnp.zeros((D_MODEL, c), F32)
    w_aux = jnp.concatenate([iq, ik, zeros(64), iw, zeros(LANES - IDX_HEADS)], axis=1)
    x2 = x.reshape(n, D_MODEL)
    col_scale = _query_col_scale(w_main.shape[1], aw, A_HEAD_DIM ** -0.5)
    main2, aux2, av_t = _inproj(x2, pre_gain, w_main, col_scale, w_aux, True, av, A_HEAD_DIM)
    main = main2.reshape(b, t, -1)
    aux = aux2.reshape(b, t, -1)

    topk = min(DSA_TOPK, t // 4)
    bias = _dsa_select(aux, topk)
    heads = [(h // 2, h % 2, h // 2, h * (A_HEAD_DIM + ONES_ROWS), A_HEAD_DIM, h * A_HEAD_DIM)
             for h in range(A_HEADS)]
    a = _flash(main, 0, aw, main, 1, aw, av_t, heads, bias=bias)
    r = _retention(main, posf, 3, 4, 5)
    return _outproj_even(a.reshape(n, aw), main2, 2, r.reshape(n, -1), w_out, post_gain,
                         x2).reshape(b, t, D_MODEL)


def _odd_layer(x, posf, pre_gain, post_gain, w_in, w_out, lam_params, subln, q_norm, kv_norm,
               w_uq, w_ukv, layer):
    b, t, _ = x.shape
    n = b * t
    cw = C_HEADS * C_V_DIM
    o = 0
    cq, o = _cols(w_in, o, cw), o + cw
    ck, o = _cols(w_in, o, cw), o + cw
    cv, o = _cols(w_in, o, cw), o + cw
    cg, o = _cols(w_in, o, cw), o + cw
    dcq, o = _cols(w_in, o, D_Q_LORA), o + D_Q_LORA
    dckv, o = _cols(w_in, o, D_KV_LORA), o + D_KV_LORA
    dkr, o = _cols(w_in, o, D_ROPE_DIM), o + D_ROPE_DIM
    dg, o = _cols(w_in, o, cw), o + cw
    w_main = jnp.concatenate([cq, ck, cg, dg], axis=1)
    zeros = lambda c: jnp.zeros((D_MODEL, c), F32)
    w_aux = jnp.concatenate([dcq, dckv, zeros(D_NOPE_DIM), dkr,
                             zeros(LANES - D_NOPE_DIM - D_ROPE_DIM)], axis=1)
    x2 = x.reshape(n, D_MODEL)
    col_scale = _query_col_scale(w_main.shape[1], cw, C_QK_DIM ** -0.5)
    main2, aux2, cv_t = _inproj(x2, pre_gain, w_main, col_scale, w_aux, False, cv, C_V_DIM)
    main = main2.reshape(b, t, -1)

    heads_c = [(h, m, h, h * (C_V_DIM + ONES_ROWS), C_V_DIM, (m * C_HEADS + h) * C_V_DIM)
               for m in range(2) for h in range(C_HEADS)]
    od = _flash(main, 0, cw, main, 1, cw, cv_t, heads_c)

    qd, kd, vd_t = _mla_prep(aux2, posf.reshape(n, 1), q_norm, kv_norm, w_uq, w_ukv)
    heads_d = [(h, None, h, h * (D_V_DIM + ONES_ROWS), D_V_DIM, h * D_V_DIM)
               for h in range(D_HEADS)]
    om = _flash(qd.reshape(b, t, -1), 0, D_HEADS * LANES, kd.reshape(b, t, -1), 0,
                D_HEADS * LANES, vd_t, heads_d)

    lam_init = 0.8 - 0.6 * math.exp(-0.3 * layer)
    return _outproj_odd(od.reshape(n, -1), main2, 2, om.reshape(n, -1), 3, lam_params, subln,
                        w_out, post_gain, x2, lam_init).reshape(b, t, D_MODEL)


def kernel(x, positions, pre_norm, post_norm, w_in_even, w_out_even, w_in_odd, diff_lambda,
           diff_subln, mla_q_norm, mla_kv_norm, mla_w_uq, mla_w_ukv, w_out_odd):
    b, t, _ = x.shape
    posf = positions.astype(F32).reshape(b, t, 1)
    depth = pre_norm.shape[0]
    for layer in range(depth):
        j = layer // 2
        if layer % 2 == 0:
            x = _even_layer(x, posf, pre_norm[layer], post_norm[layer], w_in_even[j],
                            w_out_even[j])
        else:
            x = _odd_layer(x, posf, pre_norm[layer], post_norm[layer], w_in_odd[j],
                           w_out_odd[j], diff_lambda[j], diff_subln[j], mla_q_norm[j],
                           mla_kv_norm[j], mla_w_uq[j], mla_w_ukv[j], layer)
    return x
```
